```python
import math
import jax, jax.numpy as jnp
from jax import lax
import numpy as np

D_MODEL = 1024
BATCH = 8
SEQ = 8192
DEPTH = 2
DEC_BATCH = 8
DEC_SEQ = 32
PAST_LEN = 1024

CHUNK = 64
D_MIX = D_MODEL
CONV_DIM = D_MIX // 2
CONV_WIDTH = 3
N_HEADS = 8
HEAD_DIM = (D_MIX - CONV_DIM) // N_HEADS
N_KV_HEADS = 2
GROUP = N_HEADS // N_KV_HEADS
N_IDX_HEADS = 8
IDX_DIM = 64
TOPK_MAX = 256
N_BUCKETS = 32
REL_MAX_DIST = 128
D_FF = 4 * D_MODEL
Q_BLOCK = 128
ALPHA = (2 * DEPTH) ** 0.25
BETA = (8 * DEPTH) ** -0.25
LN_EPS = 1e-5
NEG = -1e30
SPLIT_SIZES = (CONV_DIM, CONV_DIM, CONV_DIM, N_HEADS * HEAD_DIM, N_KV_HEADS * HEAD_DIM,
               N_KV_HEADS * HEAD_DIM, N_IDX_HEADS * IDX_DIM, IDX_DIM, N_IDX_HEADS)
D_IN_PROJ = sum(SPLIT_SIZES)

kernel_name = "hybrid_conv_dsa_streaming_step"


def _split_points():
    pts, acc = [], 0
    for s in SPLIT_SIZES[:-1]:
        acc += s
        pts.append(acc)
    return pts


def _layernorm(x, g, b):
    xf = x.astype(jnp.float32)
    mu = jnp.mean(xf, axis=-1, keepdims=True)
    xc = xf - mu
    var = jnp.mean(xc * xc, axis=-1, keepdims=True)
    y = xc * lax.rsqrt(var + LN_EPS) * g.astype(jnp.float32) + b.astype(jnp.float32)
    return y.astype(x.dtype)


def _t5_bucket(rel):
    nb = N_BUCKETS // 2
    ret = (rel > 0).astype(jnp.int32) * nb
    n = jnp.abs(rel)
    max_exact = nb // 2
    nf = jnp.maximum(n, 1).astype(jnp.float32)
    large = max_exact + (jnp.log(nf / max_exact) / math.log(REL_MAX_DIST / max_exact)
                         * (nb - max_exact)).astype(jnp.int32)
    large = jnp.minimum(large, nb - 1)
    return ret + jnp.where(n < max_exact, n, large)


def _gather_rows(a, idx):
    return jax.vmap(lambda ab, ib: ab[ib])(a, idx)


def _sparse_attend(q, qi, wi, k, v, ki, q_pos, k_top, rel_bias):
    Bn, T = q.shape[0], q.shape[1]
    L = k.shape[1]
    f32 = jnp.float32
    s = jnp.einsum('btnd,bld->btnl', qi.astype(f32), ki.astype(f32)) * (IDX_DIM ** -0.5)
    score = jnp.einsum('btnl,btn->btl', jax.nn.relu(s), wi.astype(f32))
    limit = jnp.minimum((q_pos // CHUNK + 1) * CHUNK, L)
    admissible = jnp.arange(L, dtype=jnp.int32)[None, :] < limit[:, None]
    score = jnp.where(admissible[None], score, -jnp.inf)
    _, idx = lax.top_k(score, k_top)
    valid = idx < limit[None, :, None]
    k_sel = _gather_rows(k, idx).astype(f32)
    v_sel = _gather_rows(v, idx).astype(f32)
    qg = q.reshape(Bn, T, N_KV_HEADS, GROUP, HEAD_DIM).astype(f32)
    logits = jnp.einsum('btgrd,btkgd->btgrk', qg, k_sel) * (HEAD_DIM ** -0.5)
    bucket = _t5_bucket(idx - q_pos[None, :, None])
    bias = rel_bias.astype(f32)[bucket]
    bias = bias.reshape(Bn, T, k_top, N_KV_HEADS, GROUP).transpose(0, 1, 3, 4, 2)
    logits = jnp.where(valid[:, :, None, None, :], logits + bias, NEG)
    p = jax.nn.softmax(logits, axis=-1)
    out = jnp.einsum('btgrk,btkgd->btgrd', p, v_sel)
    return out.reshape(Bn, T, N_HEADS * HEAD_DIM).astype(q.dtype)


def _attention(q, qi, wi, k, v, ki, q_pos, k_top, rel_bias):
    Bn, T = q.shape[0], q.shape[1]
    if T > Q_BLOCK and T % Q_BLOCK == 0:
        nb = T // Q_BLOCK
        def blk(a):
            return jnp.swapaxes(a.reshape((Bn, nb, Q_BLOCK) + a.shape[2:]), 0, 1)
        xs = (blk(q), blk(qi), blk(wi), q_pos.reshape(nb, Q_BLOCK))
        out = lax.map(lambda t: _sparse_attend(t[0], t[1], t[2], k, v, ki, t[3], k_top, rel_bias), xs)
        return jnp.swapaxes(out, 0, 1).reshape(Bn, T, N_HEADS * HEAD_DIM)
    return _sparse_attend(q, qi, wi, k, v, ki, q_pos, k_top, rel_bias)


def _layer(x, conv_prev, k_past, v_past, ki_past, pos0,
           w_in, conv_w, w_o, ln1_g, ln1_b, w_ff1, w_ff2, ln2_g, ln2_b, rel_bias):
    Bn, T, _ = x.shape
    proj = x @ w_in
    gb, gc, h, q, k, v, qi, ki, wi = jnp.split(proj, _split_points(), axis=-1)
    u = gc * h
    u_pad = jnp.concatenate([conv_prev.astype(u.dtype), u], axis=1)
    y = conv_w[0] * u_pad[:, 0:T]
    for j in range(1, CONV_WIDTH):
        y = y + conv_w[j] * u_pad[:, j:j + T]
    conv_out = gb * y
    new_conv = u_pad[:, -(CONV_WIDTH - 1):]
    q = q.reshape(Bn, T, N_HEADS, HEAD_DIM)
    k = k.reshape(Bn, T, N_KV_HEADS, HEAD_DIM)
    v = v.reshape(Bn, T, N_KV_HEADS, HEAD_DIM)
    qi = qi.reshape(Bn, T, N_IDX_HEADS, IDX_DIM)
    wi = wi * (N_IDX_HEADS ** -0.5)
    k_all = jnp.concatenate([k_past.astype(k.dtype), k], axis=1)
    v_all = jnp.concatenate([v_past.astype(v.dtype), v], axis=1)
    ki_all = jnp.concatenate([ki_past.astype(ki.dtype), ki], axis=1)
    L = k_all.shape[1]
    k_top = min(TOPK_MAX, L // 4)
    q_pos = pos0 + jnp.arange(T, dtype=jnp.int32)
    attn = _attention(q, qi, wi, k_all, v_all, ki_all, q_pos, k_top, rel_bias)
    mix = jnp.concatenate([conv_out, attn], axis=-1) @ w_o
    x = _layernorm(ALPHA * x + mix, ln1_g, ln1_b)
    hid = jax.nn.relu(x @ w_ff1)
    ff = (hid * hid) @ w_ff2
    x = _layernorm(ALPHA * x + ff, ln2_g, ln2_b)
    return x, k, v, ki, new_conv


def setup_inputs(seed: int = 0) -> dict:
    key = jax.random.key(seed)
    ks = jax.random.split(key, 20)
    f32 = jnp.float32
    nrm = lambda k, shape, s: jax.random.normal(k, shape, f32) * s
    return {
        "x_prompt": nrm(ks[0], (BATCH, SEQ, D_MODEL), 1.0),
        "x_sample": nrm(ks[1], (DEC_BATCH, DEC_SEQ, D_MODEL), 1.0),
        "cache_k": nrm(ks[2], (DEPTH, DEC_BATCH, PAST_LEN, N_KV_HEADS, HEAD_DIM), 1.0),
        "cache_v": nrm(ks[3], (DEPTH, DEC_BATCH, PAST_LEN, N_KV_HEADS, HEAD_DIM), 1.0),
        "cache_kidx": nrm(ks[4], (DEPTH, DEC_BATCH, PAST_LEN, IDX_DIM), 1.0),
        "state_conv": nrm(ks[5], (DEPTH, DEC_BATCH, CONV_WIDTH - 1, CONV_DIM), 1.0),
        "w_in": nrm(ks[6], (DEPTH, D_MODEL, D_IN_PROJ), D_MODEL ** -0.5),
        "conv_w": nrm(ks[7], (DEPTH, CONV_WIDTH, CONV_DIM), CONV_WIDTH ** -0.5),
        "w_o": nrm(ks[8], (DEPTH, D_MIX, D_MODEL), BETA * D_MIX ** -0.5),
        "ln1_g": 1.0 + nrm(ks[9], (DEPTH, D_MODEL), 0.02),
        "ln1_b": nrm(ks[10], (DEPTH, D_MODEL), 0.02),
        "w_ff1": nrm(ks[11], (DEPTH, D_MODEL, D_FF), D_MODEL ** -0.5),
        "w_ff2": nrm(ks[12], (DEPTH, D_FF, D_MODEL), BETA * D_FF ** -0.5),
        "ln2_g": 1.0 + nrm(ks[13], (DEPTH, D_MODEL), 0.02),
        "ln2_b": nrm(ks[14], (DEPTH, D_MODEL), 0.02),
        "rel_bias": nrm(ks[15], (N_BUCKETS, N_HEADS), 0.1),
    }


def reference(x_prompt, x_sample, cache_k, cache_v, cache_kidx, state_conv,
              w_in, conv_w, w_o, ln1_g, ln1_b, w_ff1, w_ff2, ln2_g, ln2_b, rel_bias):
    hp, hs = x_prompt, x_sample
    Bp = x_prompt.shape[0]
    pk, pv, pki, pc, sk, sv, ski, sc = [], [], [], [], [], [], [], []
    for l in range(DEPTH):
        lw = (w_in[l], conv_w[l], w_o[l], ln1_g[l], ln1_b[l], w_ff1[l], w_ff2[l], ln2_g[l], ln2_b[l], rel_bias)
        zero_conv = jnp.zeros((Bp, CONV_WIDTH - 1, CONV_DIM), hp.dtype)
        empty_kv = jnp.zeros((Bp, 0, N_KV_HEADS, HEAD_DIM), hp.dtype)
        empty_ki = jnp.zeros((Bp, 0, IDX_DIM), hp.dtype)
        hp, k_n, v_n, ki_n, c_n = _layer(hp, zero_conv, empty_kv, empty_kv, empty_ki, 0, *lw)
        pk.append(k_n); pv.append(v_n); pki.append(ki_n); pc.append(c_n)
        hs, k_n, v_n, ki_n, c_n = _layer(hs, state_conv[l], cache_k[l], cache_v[l], cache_kidx[l], PAST_LEN, *lw)
        sk.append(k_n); sv.append(v_n); ski.append(ki_n); sc.append(c_n)
    return (hp, hs, jnp.stack(pk), jnp.stack(pv), jnp.stack(pki), jnp.stack(pc),
            jnp.stack(sk), jnp.stack(sv), jnp.stack(ski), jnp.stack(sc))
```

```python
import functools
import math

import jax
import jax.numpy as jnp
from jax import lax
from jax.experimental import pallas as pl
from jax.experimental.pallas import tpu as pltpu

D_MODEL = 1024
CHUNK = 64
CONV_DIM = 512
CONV_WIDTH = 3
N_HEADS = 8
HEAD_DIM = 64
N_KV_HEADS = 2
GROUP = N_HEADS // N_KV_HEADS
N_IDX_HEADS = 8
IDX_DIM = 64
TOPK_MAX = 256
N_BUCKETS = 32
REL_MAX_DIST = 128
D_FF = 4 * D_MODEL
DEPTH = 2
ALPHA = (2 * DEPTH) ** 0.25
LN_EPS = 1e-5
NEG = -1e30

LANES = 128
SENTINEL = float(jnp.finfo(jnp.float32).min)
M_INIT = -5e29

MXU_DTYPE = jnp.bfloat16
VMEM_LIMIT = 56 * 1024 * 1024

D_QKV = N_HEADS * HEAD_DIM + 2 * N_KV_HEADS * HEAD_DIM
D_IDX = N_IDX_HEADS * IDX_DIM + IDX_DIM + N_IDX_HEADS
D_IDX_PAD = 640


def _dot(a, b):
    return jnp.dot(a, b, preferred_element_type=jnp.float32)


def _dot_nt(a, b):
    return lax.dot_general(a, b, (((1,), (1,)), ((), ())), preferred_element_type=jnp.float32)


def _t5_bucket(rel):
    nb = N_BUCKETS // 2
    ret = (rel > 0).astype(jnp.int32) * nb
    n = jnp.abs(rel)
    max_exact = nb // 2
    nf = jnp.maximum(n, 1).astype(jnp.float32)
    large = max_exact + (jnp.log(nf / max_exact) / math.log(REL_MAX_DIST / max_exact)
                         * (nb - max_exact)).astype(jnp.int32)
    large = jnp.minimum(large, nb - 1)
    return ret + jnp.where(n < max_exact, n, large)


def _bias_kernel(rb_ref, out_ref, *, tq):
    i = lax.broadcasted_iota(jnp.int32, (tq, LANES), 0)
    j = lax.broadcasted_iota(jnp.int32, (tq, LANES), 1)
    far_bucket = _t5_bucket(jnp.full((tq, LANES), -(REL_MAX_DIST + 1), jnp.int32))
    buckets = [far_bucket, _t5_bucket(j - LANES - i), _t5_bucket(j - i)]
    for h in range(N_HEADS):
        vals = []
        for bk in buckets:
            v = jnp.zeros((tq, LANES), jnp.float32)
            for b in range(N_BUCKETS):
                v = jnp.where(bk == b, rb_ref[b, h], v)
            vals.append(v)
        out_ref[0, h] = jnp.zeros((tq, LANES), jnp.float32)
        out_ref[1, h] = vals[1] - vals[0]
        out_ref[2, h] = vals[2] - vals[0]


def _bias_tiles(rel_bias, tq):
    return pl.pallas_call(
        functools.partial(_bias_kernel, tq=tq),
        out_shape=jax.ShapeDtypeStruct((3, N_HEADS, tq, LANES), jnp.float32),
        in_specs=[pl.BlockSpec(memory_space=pltpu.SMEM)],
        name="rel_bias_tiles",
    )(rel_bias)


def _inproj_kernel(x_ref, cprev_ref, wa_ref, wb_ref, wc_ref, cw_ref,
                   co_ref, q_ref, kg_ref, vg_ref, k_ref, v_ref, qi_ref, ki_ref, kib_ref, w_ref, nc_ref,
                   ubuf, *, tm, n_t):
    t = pl.program_id(1)
    xb = x_ref[0].astype(MXU_DTYPE)

    pa = _dot(xb, wa_ref[...])
    gb = pa[:, :CONV_DIM]
    u = pa[:, CONV_DIM:2 * CONV_DIM] * pa[:, 2 * CONV_DIM:]

    @pl.when(t == 0)
    def _():
        ubuf[0:8, :] = jnp.zeros((8, CONV_DIM), jnp.float32)
        ubuf[6:8, :] = cprev_ref[0]

    ubuf[8:8 + tm, :] = u
    cw = cw_ref[...]
    y = cw[0:1, :] * ubuf[6:6 + tm, :] + cw[1:2, :] * ubuf[7:7 + tm, :]
    y = y + cw[2:3, :] * u
    co_ref[0] = (gb * y).astype(co_ref.dtype)

    @pl.when(t == n_t - 1)
    def _():
        nc_ref[0] = ubuf[tm + 6:tm + 8, :]

    ubuf[0:8, :] = ubuf[tm:tm + 8, :]

    pb = _dot(xb, wb_ref[...])
    q = pb[:, :N_HEADS * HEAD_DIM] * (HEAD_DIM ** -0.5)
    for h in range(N_HEADS):
        q_ref[0, h] = q[:, h * HEAD_DIM:(h + 1) * HEAD_DIM].astype(q_ref.dtype)
    k = pb[:, N_HEADS * HEAD_DIM:N_HEADS * HEAD_DIM + N_KV_HEADS * HEAD_DIM]
    v = pb[:, N_HEADS * HEAD_DIM + N_KV_HEADS * HEAD_DIM:]
    k_ref[0] = k
    v_ref[0] = v
    for g in range(N_KV_HEADS):
        kg_ref[0, g] = k[:, g * HEAD_DIM:(g + 1) * HEAD_DIM].astype(kg_ref.dtype)
        vg_ref[0, g] = v[:, g * HEAD_DIM:(g + 1) * HEAD_DIM].astype(vg_ref.dtype)

    pc = _dot(xb, wc_ref[...])
    for h in range(N_IDX_HEADS):
        qi_ref[0, h] = pc[:, h * IDX_DIM:(h + 1) * IDX_DIM].astype(qi_ref.dtype)
    ki = pc[:, N_IDX_HEADS * IDX_DIM:N_IDX_HEADS * IDX_DIM + IDX_DIM]
    ki_ref[0] = ki
    kib_ref[0] = ki.astype(kib_ref.dtype)
    wi = pc[:, N_IDX_HEADS * IDX_DIM + IDX_DIM:N_IDX_HEADS * IDX_DIM + IDX_DIM + N_IDX_HEADS]
    w_ref[0] = wi * (N_IDX_HEADS ** -0.5) * (IDX_DIM ** -0.5)


def _inproj(x, conv_prev, wa, wb, wc, conv_w, *, tm):
    bsz, seq, _ = x.shape
    n_t = seq // tm
    f32 = jnp.float32
    const = lambda b, t: (0, 0)
    out_shape = (
        jax.ShapeDtypeStruct((bsz, seq, CONV_DIM), MXU_DTYPE),
        jax.ShapeDtypeStruct((bsz, N_HEADS, seq, HEAD_DIM), MXU_DTYPE),
        jax.ShapeDtypeStruct((bsz, N_KV_HEADS, seq, HEAD_DIM), MXU_DTYPE),
        jax.ShapeDtypeStruct((bsz, N_KV_HEADS, seq, HEAD_DIM), MXU_DTYPE),
        jax.ShapeDtypeStruct((bsz, seq, N_KV_HEADS * HEAD_DIM), f32),
        jax.ShapeDtypeStruct((bsz, seq, N_KV_HEADS * HEAD_DIM), f32),
        jax.ShapeDtypeStruct((bsz, N_IDX_HEADS, seq, IDX_DIM), MXU_DTYPE),
        jax.ShapeDtypeStruct((bsz, seq, IDX_DIM), f32),
        jax.ShapeDtypeStruct((bsz, seq, IDX_DIM), MXU_DTYPE),
        jax.ShapeDtypeStruct((bsz, seq, N_IDX_HEADS), f32),
        jax.ShapeDtypeStruct((bsz, CONV_WIDTH - 1, CONV_DIM), f32),
    )
    tok = lambda d: pl.BlockSpec((1, tm, d), lambda b, t: (b, t, 0))
    hm = lambda n, d: pl.BlockSpec((1, n, tm, d), lambda b, t: (b, 0, t, 0))
    state = pl.BlockSpec((1, CONV_WIDTH - 1, CONV_DIM), lambda b, t: (b, 0, 0))
    return pl.pallas_call(
        functools.partial(_inproj_kernel, tm=tm, n_t=n_t),
        out_shape=out_shape,
        grid=(bsz, n_t),
        in_specs=[
            tok(D_MODEL), state,
            pl.BlockSpec((D_MODEL, 3 * CONV_DIM), const),
            pl.BlockSpec((D_MODEL, D_QKV), const),
            pl.BlockSpec((D_MODEL, D_IDX_PAD), const),
            pl.BlockSpec((CONV_WIDTH, CONV_DIM), const),
        ],
        out_specs=(
            tok(CONV_DIM), hm(N_HEADS, HEAD_DIM), hm(N_KV_HEADS, HEAD_DIM), hm(N_KV_HEADS, HEAD_DIM),
            tok(N_KV_HEADS * HEAD_DIM), tok(N_KV_HEADS * HEAD_DIM), hm(N_IDX_HEADS, IDX_DIM),
            tok(IDX_DIM), tok(IDX_DIM), tok(N_IDX_HEADS), state,
        ),
        scratch_shapes=[pltpu.VMEM((tm + 8, CONV_DIM), f32)],
        compiler_params=pltpu.CompilerParams(
            dimension_semantics=("arbitrary", "arbitrary"), vmem_limit_bytes=VMEM_LIMIT),
        name="inproj_conv",
    )(x, conv_prev, wa, wb, wc, conv_w)


def _key_to_f32(key):
    bits = jnp.where(key < 0, key & jnp.int32(0x7FFFFFFF), ~key)
    return lax.bitcast_convert_type(bits, jnp.float32)


def _attn_kernel(qi_ref, q_ref, w_ref, ki_ref, kg_ref, vg_ref, nb_ref, out_ref,
                 sc_ref, wb_ref, m_ref, l_ref, acc_ref, *, tq, cb, pos0, n_keys, k_top):
    ck = cb * LANES
    p0 = pos0 + pl.program_id(1) * tq
    n_blk = p0 // LANES + 1
    n_chunk = (n_blk + cb - 1) // cb
    q_pos = p0 + lax.broadcasted_iota(jnp.int32, (tq, 1), 0)
    limit = jnp.minimum((q_pos // CHUNK + 1) * CHUNK, n_keys)
    lane = lax.broadcasted_iota(jnp.int32, (tq, LANES), 1)

    w = w_ref[0]
    for h in range(N_IDX_HEADS):
        wb_ref[h] = jnp.broadcast_to(w[:, h:h + 1], (tq, LANES))
    qi_all = qi_ref[0].reshape(N_IDX_HEADS * tq, IDX_DIM)

    def score_chunk(c, carry):
        s0 = pl.multiple_of(c * ck, ck)
        s = _dot_nt(qi_all, ki_ref[0, pl.ds(s0, ck), :])
        for j in range(cb):
            acc = jnp.zeros((tq, LANES), jnp.float32)
            for h in range(N_IDX_HEADS):
                blk = s[h * tq:(h + 1) * tq, j * LANES:(j + 1) * LANES]
                acc = acc + jnp.maximum(blk, 0.0) * wb_ref[h]
            col = s0 + j * LANES + lane
            sc_ref[c * cb + j] = jnp.where(col < limit, acc, SENTINEL)
        return carry

    lax.fori_loop(0, n_chunk, score_chunk, 0)

    def count_ge(thr):
        def body(c, cnt):
            for j in range(cb):
                cnt = cnt + jnp.where(sc_ref[c * cb + j] >= thr, 1, 0)
            return cnt
        cnt = lax.fori_loop(0, n_chunk, body, jnp.zeros((tq, LANES), jnp.int32))
        return jnp.sum(cnt, axis=1, keepdims=True)

    def search_step(it, carry):
        key, cnt_at_key = carry
        cand = key | lax.shift_left(jnp.int32(1), 31 - it)
        cnt = count_ge(_key_to_f32(cand))
        ok = cnt >= k_top
        return jnp.where(ok, cand, key), jnp.where(ok, cnt, cnt_at_key)

    key0 = jnp.zeros((tq, 1), jnp.int32)
    key, cnt_ge = lax.fori_loop(0, 32, search_step, (key0, key0))
    thr = _key_to_f32(key)
    sent_next = jnp.nextafter(jnp.float32(SENTINEL), jnp.float32(0.0))
    has_k = thr > SENTINEL
    thr = jnp.maximum(thr, sent_next)

    excess = jnp.where(has_k & (cnt_ge > k_top), 1, 0)

    @pl.when(jnp.sum(excess) > 0)
    def _():
        def count_gt(c, cnt):
            for j in range(cb):
                cnt = cnt + jnp.where(sc_ref[c * cb + j] > thr, 1, 0)
            return cnt
        cnt_gt = lax.fori_loop(0, n_chunk, count_gt, jnp.zeros((tq, LANES), jnp.int32))
        need = k_top - jnp.sum(cnt_gt, axis=1, keepdims=True)

        def ties_before(bound):
            def body(c, cnt):
                for j in range(cb):
                    col = (c * cb + j) * LANES + lane
                    cnt = cnt + jnp.where((sc_ref[c * cb + j] == thr) & (col < bound), 1, 0)
                return cnt
            cnt = lax.fori_loop(0, n_chunk, body, jnp.zeros((tq, LANES), jnp.int32))
            return jnp.sum(cnt, axis=1, keepdims=True)

        n_bits = max(1, (n_keys + ck).bit_length())

        def cut_step(it, cut):
            cand = cut | lax.shift_left(jnp.int32(1), n_bits - 1 - it)
            return jnp.where(ties_before(cand) <= need, cand, cut)

        cut = lax.fori_loop(0, n_bits, cut_step, jnp.zeros((tq, 1), jnp.int32))

        def drop(c, carry):
            for j in range(cb):
                col = (c * cb + j) * LANES + lane
                sc = sc_ref[c * cb + j]
                sc_ref[c * cb + j] = jnp.where((sc == thr) & (col >= cut), SENTINEL, sc)
            return carry
        lax.fori_loop(0, n_chunk, drop, 0)

    m_ref[...] = jnp.full(m_ref.shape, M_INIT, jnp.float32)
    l_ref[...] = jnp.zeros(l_ref.shape, jnp.float32)
    acc_ref[...] = jnp.zeros(acc_ref.shape, jnp.float32)

    def attend_chunk(c, near):
        s0 = pl.multiple_of(c * ck, ck)
        sel = [sc_ref[c * cb + j] >= thr for j in range(cb)]
        for g in range(N_KV_HEADS):
            qg = q_ref[0, g * GROUP:(g + 1) * GROUP].reshape(GROUP * tq, HEAD_DIM)
            lg = _dot_nt(qg, kg_ref[0, g, pl.ds(s0, ck), :])
            blocks = []
            for j in range(cb):
                blk = lg[:, j * LANES:(j + 1) * LANES].reshape(GROUP, tq, LANES)
                if near:
                    bias_idx = jnp.clip(c * cb + j - (n_blk - 3), 0, 2)
                    blk = blk + nb_ref[bias_idx, g * GROUP:(g + 1) * GROUP]
                blocks.append(jnp.where(sel[j][None], blk, NEG))
            m_old = m_ref[g]
            m_new = m_old
            for blk in blocks:
                m_new = jnp.maximum(m_new, jnp.max(blk, axis=2, keepdims=True))
            scale = jnp.exp(m_old - m_new)
            probs = [jnp.exp(blk - m_new) for blk in blocks]
            row_sum = probs[0].sum(axis=2, keepdims=True)
            for p in probs[1:]:
                row_sum = row_sum + p.sum(axis=2, keepdims=True)
            l_ref[g] = scale * l_ref[g] + row_sum
            m_ref[g] = m_new
            pmat = jnp.concatenate([p.reshape(GROUP * tq, LANES) for p in probs], axis=1).astype(MXU_DTYPE)
            pv = _dot(pmat, vg_ref[0, g, pl.ds(s0, ck), :])
            acc_ref[g] = scale.reshape(GROUP * tq, 1) * acc_ref[g] + pv

    n_far = jnp.maximum(n_chunk - 2, 0)

    def far_body(c, carry):
        attend_chunk(c, near=False)
        return carry

    def near_body(c, carry):
        attend_chunk(c, near=True)
        return carry

    lax.fori_loop(0, n_far, far_body, 0)
    lax.fori_loop(n_far, n_chunk, near_body, 0)

    for g in range(N_KV_HEADS):
        o = acc_ref[g] / l_ref[g].reshape(GROUP * tq, 1)
        for r in range(GROUP):
            h = g * GROUP + r
            out_ref[0, :, h * HEAD_DIM:(h + 1) * HEAD_DIM] = o[r * tq:(r + 1) * tq].astype(out_ref.dtype)


def _attention(qi, q, w, ki, kg, vg, nb, *, tq, cb, pos0, n_keys, k_top):
    bsz, _, seq, _ = q.shape
    l_pad = ki.shape[1]
    n_q = seq // tq
    max_blk = (pos0 + seq - tq) // LANES + 1
    n_blk_pad = -(-max_blk // cb) * cb
    assert pos0 % LANES == 0 and tq <= LANES and (tq == LANES or n_q == 1)
    assert n_blk_pad * LANES <= l_pad, (n_blk_pad, l_pad)
    qspec = lambda n, d: pl.BlockSpec((1, n, tq, d), lambda b, i: (b, 0, i, 0))
    kspec = lambda n, d: pl.BlockSpec((1, n, l_pad, d), lambda b, i: (b, 0, 0, 0))
    return pl.pallas_call(
        functools.partial(_attn_kernel, tq=tq, cb=cb, pos0=pos0, n_keys=n_keys, k_top=k_top),
        out_shape=jax.ShapeDtypeStruct((bsz, seq, N_HEADS * HEAD_DIM), MXU_DTYPE),
        grid=(bsz, n_q),
        in_specs=[
            qspec(N_IDX_HEADS, IDX_DIM), qspec(N_HEADS, HEAD_DIM),
            pl.BlockSpec((1, tq, N_IDX_HEADS), lambda b, i: (b, i, 0)),
            pl.BlockSpec((1, l_pad, IDX_DIM), lambda b, i: (b, 0, 0)),
            kspec(N_KV_HEADS, HEAD_DIM), kspec(N_KV_HEADS, HEAD_DIM),
            pl.BlockSpec((3, N_HEADS, tq, LANES), lambda b, i: (0, 0, 0, 0)),
        ],
        out_specs=pl.BlockSpec((1, tq, N_HEADS * HEAD_DIM), lambda b, i: (b, i, 0)),
        scratch_shapes=[
            pltpu.VMEM((n_blk_pad, tq, LANES), jnp.float32),
            pltpu.VMEM((N_IDX_HEADS, tq, LANES), jnp.float32),
            pltpu.VMEM((N_KV_HEADS, GROUP, tq, 1), jnp.float32),
            pltpu.VMEM((N_KV_HEADS, GROUP, tq, 1), jnp.float32),
            pltpu.VMEM((N_KV_HEADS, GROUP * tq, HEAD_DIM), jnp.float32),
        ],
        compiler_params=pltpu.CompilerParams(
            dimension_semantics=("arbitrary", "arbitrary"), vmem_limit_bytes=VMEM_LIMIT),
        name="dsa_attention",
    )(qi, q, w, ki, kg, vg, nb)


def _layernorm(x, g, b):
    mu = jnp.mean(x, axis=-1, keepdims=True)
    xc = x - mu
    var = jnp.mean(xc * xc, axis=-1, keepdims=True)
    return xc * lax.rsqrt(var + LN_EPS) * g + b


def _mlp_kernel(x_ref, co_ref, at_ref, woa_ref, wob_ref, g1_ref, b1_ref, w1_ref, w2_ref, g2_ref, b2_ref,
                out_ref, *, ff_chunk):
    mix = _dot(co_ref[...], woa_ref[...]) + _dot(at_ref[...], wob_ref[...])
    x1 = _layernorm(ALPHA * x_ref[...] + mix, g1_ref[...], b1_ref[...])
    x1b = x1.astype(MXU_DTYPE)
    ff = jnp.zeros(x1.shape, jnp.float32)
    for c in range(D_FF // ff_chunk):
        hid = jnp.maximum(_dot(x1b, w1_ref[:, c * ff_chunk:(c + 1) * ff_chunk]), 0.0)
        ff = ff + _dot((hid * hid).astype(MXU_DTYPE), w2_ref[c * ff_chunk:(c + 1) * ff_chunk, :])
    out_ref[...] = _layernorm(ALPHA * x1 + ff, g2_ref[...], b2_ref[...])


def _mlp(x, conv_out, attn, woa, wob, g1, b1, w1, w2, g2, b2, *, tm, ff_chunk=1024):
    n_tok = x.shape[0]
    tok = lambda d: pl.BlockSpec((tm, d), lambda i: (i, 0))
    const = lambda shape: pl.BlockSpec(shape, lambda i: (0, 0), pipeline_mode=pl.Buffered(1))
    return pl.pallas_call(
        functools.partial(_mlp_kernel, ff_chunk=ff_chunk),
        out_shape=jax.ShapeDtypeStruct((n_tok, D_MODEL), jnp.float32),
        grid=(n_tok // tm,),
        in_specs=[
            tok(D_MODEL), tok(CONV_DIM), tok(N_HEADS * HEAD_DIM),
            const((CONV_DIM, D_MODEL)), const((N_HEADS * HEAD_DIM, D_MODEL)),
            const((1, D_MODEL)), const((1, D_MODEL)),
            const((D_MODEL, D_FF)), const((D_FF, D_MODEL)),
            const((1, D_MODEL)), const((1, D_MODEL)),
        ],
        out_specs=tok(D_MODEL),
        compiler_params=pltpu.CompilerParams(
            dimension_semantics=("arbitrary",), vmem_limit_bytes=VMEM_LIMIT),
        name="oproj_mlp",
    )(x, conv_out, attn, woa, wob, g1, b1, w1, w2, g2, b2)


def _layer(x, conv_prev, past, pos0, lw, nb, *, tm_in, tq, cb, tm_mlp):
    bsz, seq, _ = x.shape
    wa, wb, wc, conv_w, woa, wob, g1, b1, w1, w2, g2, b2 = lw
    (conv_out, q, kg, vg, k, v, qi, ki, kib, w, new_conv) = _inproj(x, conv_prev, wa, wb, wc, conv_w, tm=tm_in)
    if past is not None:
        kg = jnp.concatenate([past[0], kg], axis=2)
        vg = jnp.concatenate([past[1], vg], axis=2)
        kib = jnp.concatenate([past[2], kib], axis=1)
    n_keys = kg.shape[2]
    k_top = min(TOPK_MAX, n_keys // 4)
    ck = cb * LANES
    max_blk = (pos0 + seq - tq) // LANES + 1
    l_pad = max(-(-max_blk // cb) * cb * LANES, -(-n_keys // ck) * ck)
    if l_pad > n_keys:
        pad = l_pad - n_keys
        kg = jnp.pad(kg, ((0, 0), (0, 0), (0, pad), (0, 0)))
        vg = jnp.pad(vg, ((0, 0), (0, 0), (0, pad), (0, 0)))
        kib = jnp.pad(kib, ((0, 0), (0, pad), (0, 0)))
    attn = _attention(qi, q, w, kib, kg, vg, nb, tq=tq, cb=cb, pos0=pos0, n_keys=n_keys, k_top=k_top)
    y = _mlp(x.reshape(bsz * seq, D_MODEL), conv_out.reshape(bsz * seq, CONV_DIM),
             attn.reshape(bsz * seq, N_HEADS * HEAD_DIM), woa, wob, g1, b1, w1, w2, g2, b2, tm=tm_mlp)
    return (y.reshape(bsz, seq, D_MODEL),
            k.reshape(bsz, seq, N_KV_HEADS, HEAD_DIM), v.reshape(bsz, seq, N_KV_HEADS, HEAD_DIM), ki, new_conv)


def _layer_weights(l, w_in, conv_w, w_o, ln1_g, ln1_b, w_ff1, w_ff2, ln2_g, ln2_b):
    wi = w_in[l].astype(MXU_DTYPE)
    wa = wi[:, :3 * CONV_DIM]
    wb = wi[:, 3 * CONV_DIM:3 * CONV_DIM + D_QKV]
    wc = jnp.pad(wi[:, 3 * CONV_DIM + D_QKV:], ((0, 0), (0, D_IDX_PAD - D_IDX)))
    wo = w_o[l].astype(MXU_DTYPE)
    row = lambda a: a[l].reshape(1, D_MODEL)
    return (wa, wb, wc, conv_w[l], wo[:CONV_DIM], wo[CONV_DIM:], row(ln1_g), row(ln1_b),
            w_ff1[l].astype(MXU_DTYPE), w_ff2[l].astype(MXU_DTYPE), row(ln2_g), row(ln2_b))


def kernel(x_prompt, x_sample, cache_k, cache_v, cache_kidx, state_conv, w_in, conv_w, w_o, ln1_g, ln1_b,
           w_ff1, w_ff2, ln2_g, ln2_b, rel_bias):
    bp, seq_p, _ = x_prompt.shape
    bs, seq_s, _ = x_sample.shape
    past_len = cache_k.shape[2]
    tq_p = LANES
    tq_s = seq_s
    nb_p = _bias_tiles(rel_bias, tq_p)
    nb_s = nb_p[:, :, :tq_s, :]
    hp, hs = x_prompt, x_sample
    outs = [[] for _ in range(8)]
    for l in range(DEPTH):
        lw = _layer_weights(l, w_in, conv_w, w_o, ln1_g, ln1_b, w_ff1, w_ff2, ln2_g, ln2_b)
        zero_conv = jnp.zeros((bp, CONV_WIDTH - 1, CONV_DIM), jnp.float32)
        hp, k_n, v_n, ki_n, c_n = _layer(hp, zero_conv, None, 0, lw, nb_p,
                                         tm_in=512, tq=tq_p, cb=4, tm_mlp=512)
        for o, a in zip(outs[:4], (k_n, v_n, ki_n, c_n)):
            o.append(a)
        past = (jnp.transpose(cache_k[l], (0, 2, 1, 3)).astype(MXU_DTYPE),
                jnp.transpose(cache_v[l], (0, 2, 1, 3)).astype(MXU_DTYPE),
                cache_kidx[l].astype(MXU_DTYPE))
        hs, k_n, v_n, ki_n, c_n = _layer(hs, state_conv[l], past, past_len, lw, nb_s,
                                         tm_in=seq_s, tq=tq_s, cb=4, tm_mlp=bs * seq_s)
        for o, a in zip(outs[4:], (k_n, v_n, ki_n, c_n)):
            o.append(a)
    return (hp, hs) + tuple(jnp.stack(o) for o in outs)
```

```python
import functools
import math

import jax
import jax.numpy as jnp
from jax import lax
from jax.experimental import pallas as pl
from jax.experimental.pallas import tpu as pltpu

D_MODEL = 1024
CHUNK = 64
CONV_DIM = 512
CONV_WIDTH = 3
N_HEADS = 8
HEAD_DIM = 64
N_KV_HEADS = 2
GROUP = N_HEADS // N_KV_HEADS
N_IDX_HEADS = 8
IDX_DIM = 64
TOPK_MAX = 256
N_BUCKETS = 32
REL_MAX_DIST = 128
D_FF = 4 * D_MODEL
DEPTH = 2
ALPHA = (2 * DEPTH) ** 0.25
LN_EPS = 1e-5
NEG = -1e30

LANES = 128
KEY_CHUNK = 512
V_ROWS = HEAD_DIM + 16
SENTINEL = float(jnp.finfo(jnp.float32).min)
M_INIT = -5e29
N_COUNTERS = 4

MXU_DTYPE = jnp.bfloat16
VMEM_LIMIT = 56 * 1024 * 1024

D_QKV = N_HEADS * HEAD_DIM + 2 * N_KV_HEADS * HEAD_DIM
D_IDX = N_IDX_HEADS * IDX_DIM + IDX_DIM + N_IDX_HEADS
D_IDX_PAD = 640


def _dot(a, b):
    return jnp.dot(a, b, preferred_element_type=jnp.float32)


def _dot_nt(a, b):
    return lax.dot_general(a, b, (((1,), (1,)), ((), ())), preferred_element_type=jnp.float32)


def _t5_bucket(rel):
    nb = N_BUCKETS // 2
    ret = (rel > 0).astype(jnp.int32) * nb
    n = jnp.abs(rel)
    max_exact = nb // 2
    nf = jnp.maximum(n, 1).astype(jnp.float32)
    large = max_exact + (jnp.log(nf / max_exact) / math.log(REL_MAX_DIST / max_exact)
                         * (nb - max_exact)).astype(jnp.int32)
    large = jnp.minimum(large, nb - 1)
    return ret + jnp.where(n < max_exact, n, large)


def _bias_kernel(rb_ref, out_ref):
    j = lax.broadcasted_iota(jnp.int32, (LANES, LANES), 0)
    i = lax.broadcasted_iota(jnp.int32, (LANES, LANES), 1)
    far_bucket = _t5_bucket(jnp.full((LANES, LANES), -(REL_MAX_DIST + 1), jnp.int32))
    buckets = [far_bucket, _t5_bucket(j - LANES - i), _t5_bucket(j - i)]
    for h in range(N_HEADS):
        vals = []
        for bk in buckets:
            v = jnp.zeros((LANES, LANES), jnp.float32)
            for b in range(N_BUCKETS):
                v = jnp.where(bk == b, rb_ref[b, h], v)
            vals.append(v)
        out_ref[0, h] = jnp.zeros((LANES, LANES), jnp.float32)
        out_ref[1, h] = vals[1] - vals[0]
        out_ref[2, h] = vals[2] - vals[0]


def _bias_tiles(rel_bias):
    return pl.pallas_call(
        _bias_kernel,
        out_shape=jax.ShapeDtypeStruct((3, N_HEADS, LANES, LANES), jnp.float32),
        in_specs=[pl.BlockSpec(memory_space=pltpu.SMEM)],
        name="rel_bias_tiles",
    )(rel_bias)


def _inproj_kernel(x_ref, cprev_ref, wa_ref, wb_ref, wc_ref, cw_ref,
                   co_ref, q_ref, kb_ref, vt_ref, k_ref, v_ref, qi_ref, ki_ref, kib_ref, wt_ref, nc_ref,
                   ubuf, *, tm, n_t, last_row):
    t = pl.program_id(1)
    xb = x_ref[0].astype(MXU_DTYPE)

    pa = _dot(xb, wa_ref[...])
    gb = pa[:, :CONV_DIM]
    u = pa[:, CONV_DIM:2 * CONV_DIM] * pa[:, 2 * CONV_DIM:]

    @pl.when(t == 0)
    def _():
        ubuf[0:8, :] = jnp.zeros((8, CONV_DIM), jnp.float32)
        ubuf[6:8, :] = cprev_ref[0]

    ubuf[8:8 + tm, :] = u
    cw = cw_ref[...]
    y = cw[0:1, :] * ubuf[6:6 + tm, :] + cw[1:2, :] * ubuf[7:7 + tm, :]
    y = y + cw[2:3, :] * u
    co_ref[0] = (gb * y).astype(co_ref.dtype)

    @pl.when(t == n_t - 1)
    def _():
        nc_ref[0] = ubuf[8 + last_row - 1:8 + last_row + 1, :]

    ubuf[0:8, :] = ubuf[tm:tm + 8, :]

    pb = _dot(xb, wb_ref[...])
    q = pb[:, :N_HEADS * HEAD_DIM] * (HEAD_DIM ** -0.5)
    zero_half = jnp.zeros((tm, HEAD_DIM), q_ref.dtype)
    for h in range(N_HEADS):
        qh = q[:, h * HEAD_DIM:(h + 1) * HEAD_DIM].astype(q_ref.dtype)
        halves = [zero_half] * N_KV_HEADS
        halves[h // GROUP] = qh
        q_ref[0, h] = jnp.concatenate(halves, axis=1)
    k = pb[:, N_HEADS * HEAD_DIM:N_HEADS * HEAD_DIM + N_KV_HEADS * HEAD_DIM]
    v = pb[:, N_HEADS * HEAD_DIM + N_KV_HEADS * HEAD_DIM:]
    k_ref[0] = k
    v_ref[0] = v
    kb_ref[0] = k.astype(kb_ref.dtype)
    vt = v.T
    ones_rows = (lax.broadcasted_iota(jnp.int32, (V_ROWS - HEAD_DIM, tm), 0) == 0).astype(vt_ref.dtype)
    for g in range(N_KV_HEADS):
        vt_ref[0, g, 0, 0:HEAD_DIM, :] = vt[g * HEAD_DIM:(g + 1) * HEAD_DIM].astype(vt_ref.dtype)
        vt_ref[0, g, 0, HEAD_DIM:V_ROWS, :] = ones_rows

    pc = _dot(xb, wc_ref[...])
    for h in range(N_IDX_HEADS):
        qi_ref[0, h] = pc[:, h * IDX_DIM:(h + 1) * IDX_DIM].astype(qi_ref.dtype)
    ki = pc[:, N_IDX_HEADS * IDX_DIM:N_IDX_HEADS * IDX_DIM + IDX_DIM]
    ki_ref[0] = ki
    kib_ref[0] = ki.astype(kib_ref.dtype)
    tail_t = pc[:, N_IDX_HEADS * IDX_DIM:].T
    wt_ref[0] = tail_t[IDX_DIM:IDX_DIM + N_IDX_HEADS] * (N_IDX_HEADS ** -0.5) * (IDX_DIM ** -0.5)


def _inproj(x, conv_prev, wa, wb, wc, conv_w, *, tm, n_valid):
    bsz, seq, _ = x.shape
    n_t = seq // tm
    f32 = jnp.float32
    const = lambda b, t: (0, 0)
    out_shape = (
        jax.ShapeDtypeStruct((bsz, seq, CONV_DIM), MXU_DTYPE),
        jax.ShapeDtypeStruct((bsz, N_HEADS, seq, N_KV_HEADS * HEAD_DIM), MXU_DTYPE),
        jax.ShapeDtypeStruct((bsz, seq, N_KV_HEADS * HEAD_DIM), MXU_DTYPE),
        jax.ShapeDtypeStruct((bsz, N_KV_HEADS, n_t, V_ROWS, tm), MXU_DTYPE),
        jax.ShapeDtypeStruct((bsz, seq, N_KV_HEADS * HEAD_DIM), f32),
        jax.ShapeDtypeStruct((bsz, seq, N_KV_HEADS * HEAD_DIM), f32),
        jax.ShapeDtypeStruct((bsz, N_IDX_HEADS, seq, IDX_DIM), MXU_DTYPE),
        jax.ShapeDtypeStruct((bsz, seq, IDX_DIM), f32),
        jax.ShapeDtypeStruct((bsz, seq, IDX_DIM), MXU_DTYPE),
        jax.ShapeDtypeStruct((bsz, N_IDX_HEADS, seq), f32),
        jax.ShapeDtypeStruct((bsz, CONV_WIDTH - 1, CONV_DIM), f32),
    )
    tok = lambda d: pl.BlockSpec((1, tm, d), lambda b, t: (b, t, 0))
    hm = lambda n, d: pl.BlockSpec((1, n, tm, d), lambda b, t: (b, 0, t, 0))
    state = pl.BlockSpec((1, CONV_WIDTH - 1, CONV_DIM), lambda b, t: (b, 0, 0))
    return pl.pallas_call(
        functools.partial(_inproj_kernel, tm=tm, n_t=n_t, last_row=n_valid - 1 - (n_t - 1) * tm),
        out_shape=out_shape,
        grid=(bsz, n_t),
        in_specs=[
            tok(D_MODEL), state,
            pl.BlockSpec((D_MODEL, 3 * CONV_DIM), const),
            pl.BlockSpec((D_MODEL, D_QKV), const),
            pl.BlockSpec((D_MODEL, D_IDX_PAD), const),
            pl.BlockSpec((CONV_WIDTH, CONV_DIM), const),
        ],
        out_specs=(
            tok(CONV_DIM), hm(N_HEADS, N_KV_HEADS * HEAD_DIM), tok(N_KV_HEADS * HEAD_DIM),
            pl.BlockSpec((1, N_KV_HEADS, 1, V_ROWS, tm), lambda b, t: (b, 0, t, 0, 0)),
            tok(N_KV_HEADS * HEAD_DIM), tok(N_KV_HEADS * HEAD_DIM), hm(N_IDX_HEADS, IDX_DIM),
            tok(IDX_DIM), tok(IDX_DIM),
            pl.BlockSpec((1, N_IDX_HEADS, tm), lambda b, t: (b, 0, t)), state,
        ),
        scratch_shapes=[pltpu.VMEM((tm + 8, CONV_DIM), f32)],
        compiler_params=pltpu.CompilerParams(
            dimension_semantics=("arbitrary", "arbitrary"), vmem_limit_bytes=VMEM_LIMIT),
        name="inproj_conv",
    )(x, conv_prev, wa, wb, wc, conv_w)


def _ordered_key(x):
    bits = lax.bitcast_convert_type(x, jnp.int32)
    return bits ^ (lax.shift_right_arithmetic(bits, 31) & jnp.int32(0x7FFFFFFF))


def _attn_kernel(qi_ref, q_ref, wt_ref, ki_ref, kb_ref, vt_ref, nb_ref, out_ref,
                 sc_ref, hi_ref, lo_ref, mma_ref, mmb_ref, lhsa_ref, lhsb_ref, rhs_ref, m_ref, acc_ref, *,
                 pos0, n_keys, k_top):
    tq, ck = LANES, KEY_CHUNK
    p0 = pos0 + pl.program_id(1) * tq
    n_blk = p0 // LANES + 1
    n_chunk = (n_blk * LANES + ck - 1) // ck
    q_pos = p0 + lax.broadcasted_iota(jnp.int32, (1, tq), 1)
    limit = jnp.minimum((q_pos // CHUNK + 1) * CHUNK, n_keys)
    row = lax.broadcasted_iota(jnp.int32, (ck, tq), 0)

    wt = wt_ref[0]
    qi_all = qi_ref[0].reshape(N_IDX_HEADS * tq, IDX_DIM)

    last = n_chunk - 1
    n_pair = (n_chunk + 1) // 2

    def idx_matmul(c, buf):
        s0 = pl.multiple_of(c * ck, ck)
        buf[...] = _dot_nt(ki_ref[0, pl.ds(s0, ck), :], qi_all)

    def idx_scores(c, buf):
        s0 = pl.multiple_of(c * ck, ck)
        acc = jnp.zeros((ck, tq), jnp.float32)
        for h in range(N_IDX_HEADS):
            acc = acc + jnp.maximum(buf[:, h * tq:(h + 1) * tq], 0.0) * wt[h:h + 1, :]
        score = jnp.where(s0 + row < limit, acc, SENTINEL)
        sc_ref[pl.ds(s0, ck), :] = score
        key = _ordered_key(score)
        hi_ref[pl.ds(s0, ck), :] = lax.shift_right_arithmetic(key, 16).astype(jnp.int16)
        lo_ref[pl.ds(s0, ck), :] = ((key & jnp.int32(0xFFFF)) - 32768).astype(jnp.int16)

    idx_matmul(0, mma_ref)

    def score_pair(p, carry):
        c0 = 2 * p
        c1 = jnp.minimum(c0 + 1, last)
        idx_matmul(c1, mmb_ref)
        idx_scores(c0, mma_ref)
        idx_matmul(jnp.minimum(c0 + 2, last), mma_ref)
        idx_scores(c1, mmb_ref)
        return carry

    lax.fori_loop(0, n_pair, score_pair, 0)

    def count_ge(ref, cand):
        thr = jnp.broadcast_to(cand, (16, tq)).astype(jnp.int16)

        def body(c, cnts):
            chunk = ref[pl.ds(pl.multiple_of(c * ck, ck), ck), :]
            cnts = list(cnts)
            for j in range(ck // 16):
                hit = jnp.where(chunk[j * 16:(j + 1) * 16] >= thr, jnp.int16(1), jnp.int16(0))
                cnts[j % N_COUNTERS] = cnts[j % N_COUNTERS] + hit
            return tuple(cnts)
        zero = jnp.zeros((16, tq), jnp.int16)
        cnts = lax.fori_loop(0, n_chunk, body, (zero,) * N_COUNTERS)
        cnt = sum(c.astype(jnp.int32) for c in cnts)
        return jnp.sum(cnt, axis=0, keepdims=True)

    def bisect(ref, cnt_all):
        def step(it, carry):
            off, cnt_at = carry
            cand = off | lax.shift_left(jnp.int32(1), 15 - it)
            cnt = count_ge(ref, cand - 32768)
            ok = cnt >= k_top
            return jnp.where(ok, cand, off), jnp.where(ok, cnt, cnt_at)
        off, cnt_at = lax.fori_loop(0, 16, step, (jnp.zeros((1, tq), jnp.int32), cnt_all))
        return off - 32768, cnt_at

    cnt_all = jnp.full((1, tq), 1, jnp.int32) * (n_chunk * ck)
    t_hi, cnt_hi = bisect(hi_ref, cnt_all)

    def pin_chunk(c, carry):
        sl = pl.ds(pl.multiple_of(c * ck, ck), ck)
        hi = hi_ref[sl, :]
        lo_ref[sl, :] = jnp.where(hi > t_hi[0:1].astype(jnp.int16), jnp.int16(32767),
                                  jnp.where(hi < t_hi[0:1].astype(jnp.int16), jnp.int16(-32768), lo_ref[sl, :]))
        return carry

    lax.fori_loop(0, n_chunk, pin_chunk, 0)
    t_lo, cnt_ge = bisect(lo_ref, cnt_hi)
    thr_key = lax.shift_left(t_hi, 16) | (t_lo + 32768)
    thr = lax.bitcast_convert_type(
        thr_key ^ (lax.shift_right_arithmetic(thr_key, 31) & jnp.int32(0x7FFFFFFF)), jnp.float32)
    sent_next = jnp.nextafter(jnp.float32(SENTINEL), jnp.float32(0.0))
    has_k = thr > SENTINEL
    thr = jnp.maximum(thr, sent_next)

    excess = jnp.where(has_k & (cnt_ge > k_top), 1, 0)

    @pl.when(jnp.sum(excess) > 0)
    def _():
        def count_gt(c, cnt):
            s0 = pl.multiple_of(c * ck, ck)
            return cnt + jnp.sum(jnp.where(sc_ref[pl.ds(s0, ck), :] > thr, 1, 0), axis=0, keepdims=True)
        need = k_top - lax.fori_loop(0, n_chunk, count_gt, jnp.zeros((1, tq), jnp.int32))

        def ties_before(bound):
            def body(c, cnt):
                s0 = pl.multiple_of(c * ck, ck)
                tie = (sc_ref[pl.ds(s0, ck), :] == thr) & (s0 + row < bound)
                return cnt + jnp.sum(jnp.where(tie, 1, 0), axis=0, keepdims=True)
            return lax.fori_loop(0, n_chunk, body, jnp.zeros((1, tq), jnp.int32))

        n_bits = max(1, (n_keys + ck).bit_length())

        def cut_step(it, cut):
            cand = cut | lax.shift_left(jnp.int32(1), n_bits - 1 - it)
            return jnp.where(ties_before(cand) <= need, cand, cut)

        cut = lax.fori_loop(0, n_bits, cut_step, jnp.zeros((1, tq), jnp.int32))

        def drop(c, carry):
            s0 = pl.multiple_of(c * ck, ck)
            sc = sc_ref[pl.ds(s0, ck), :]
            sc_ref[pl.ds(s0, ck), :] = jnp.where((sc == thr) & (s0 + row >= cut), SENTINEL, sc)
            return carry
        lax.fori_loop(0, n_chunk, drop, 0)

    m_ref[...] = jnp.full(m_ref.shape, M_INIT, jnp.float32)
    acc_ref[...] = jnp.zeros(acc_ref.shape, jnp.float32)

    eye = (lax.broadcasted_iota(jnp.int32, (tq, tq), 0) == lax.broadcasted_iota(jnp.int32, (tq, tq), 1))
    for h in range(N_HEADS):
        rhs_ref[h * tq:(h + 1) * tq, 0:LANES] = eye.astype(MXU_DTYPE)
        rhs_ref[h * tq:(h + 1) * tq, LANES:2 * LANES] = q_ref[0, h]

    def qk_matmul(c, live, lhs, buf):
        s0 = pl.multiple_of(c * ck, ck)
        sel = (sc_ref[pl.ds(s0, ck), :] >= thr) & live
        lhs[:, 0:LANES] = jnp.where(sel, 0.0, NEG).astype(MXU_DTYPE)
        lhs[:, LANES:2 * LANES] = kb_ref[0, pl.ds(s0, ck), :]
        buf[...] = _dot_nt(lhs[...], rhs_ref[...])

    def logits(buf, c, h, near):
        blk = buf[:, h * tq:(h + 1) * tq]
        if near:
            bias = [nb_ref[jnp.clip(c * (ck // LANES) + j - (n_blk - 3), 0, 2), h] for j in range(ck // LANES)]
            blk = blk + jnp.concatenate(bias, axis=0)
        return blk

    def attend_chunk(c, c_next, live_next, buf, buf_next, lhs_next, near):
        qk_matmul(c_next, live_next, lhs_next, buf_next)
        for g in range(N_KV_HEADS):
            m_old = m_ref[g]
            m_new = jnp.concatenate(
                [jnp.maximum(m_old[:, r * tq:(r + 1) * tq],
                             jnp.max(logits(buf, c, g * GROUP + r, near), axis=0, keepdims=True))
                 for r in range(GROUP)], axis=1)
            scale = jnp.exp(m_old - m_new)
            m_ref[g] = m_new
            probs = jnp.concatenate(
                [jnp.exp(logits(buf, c, g * GROUP + r, near) - m_new[:, r * tq:(r + 1) * tq]).astype(MXU_DTYPE)
                 for r in range(GROUP)], axis=1)
            pv = _dot(vt_ref[0, g, c], probs)
            acc_ref[g] = scale * acc_ref[g] + pv

    qk_matmul(0, True, lhsa_ref, mma_ref)

    def attend_pair(p, near):
        c0 = 2 * p
        c1 = jnp.minimum(c0 + 1, last)
        attend_chunk(c0, c1, c0 + 1 <= last, mma_ref, mmb_ref, lhsb_ref, near)
        attend_chunk(c1, jnp.minimum(c0 + 2, last), True, mmb_ref, mma_ref, lhsa_ref, near)

    n_far = jnp.maximum(n_pair - 2, 0)

    def far_body(p, carry):
        attend_pair(p, near=False)
        return carry

    def near_body(p, carry):
        attend_pair(p, near=True)
        return carry

    lax.fori_loop(0, n_far, far_body, 0)
    lax.fori_loop(n_far, n_pair, near_body, 0)

    for g in range(N_KV_HEADS):
        acc = acc_ref[g]
        o = acc[0:HEAD_DIM] / acc[HEAD_DIM:HEAD_DIM + 1]
        for r in range(GROUP):
            h = g * GROUP + r
            out_ref[0, :, h * HEAD_DIM:(h + 1) * HEAD_DIM] = o[:, r * tq:(r + 1) * tq].T.astype(out_ref.dtype)


def _attention(qi, q, wt, ki, kb, vt, nb, *, pos0, n_keys, k_top):
    bsz, _, seq, _ = q.shape
    tq, ck = LANES, KEY_CHUNK
    l_pad = ki.shape[1]
    n_q = seq // tq
    assert pos0 % LANES == 0 and seq % tq == 0 and l_pad % ck == 0
    assert pos0 + seq <= l_pad and n_keys <= l_pad and k_top <= ck
    qspec = lambda n, d: pl.BlockSpec((1, n, tq, d), lambda b, i: (b, 0, i, 0))
    return pl.pallas_call(
        functools.partial(_attn_kernel, pos0=pos0, n_keys=n_keys, k_top=k_top),
        out_shape=jax.ShapeDtypeStruct((bsz, seq, N_HEADS * HEAD_DIM), MXU_DTYPE),
        grid=(bsz, n_q),
        in_specs=[
            qspec(N_IDX_HEADS, IDX_DIM), qspec(N_HEADS, N_KV_HEADS * HEAD_DIM),
            pl.BlockSpec((1, N_IDX_HEADS, tq), lambda b, i: (b, 0, i)),
            pl.BlockSpec((1, l_pad, IDX_DIM), lambda b, i: (b, 0, 0)),
            pl.BlockSpec((1, l_pad, N_KV_HEADS * HEAD_DIM), lambda b, i: (b, 0, 0)),
            pl.BlockSpec((1, N_KV_HEADS, l_pad // ck, V_ROWS, ck), lambda b, i: (b, 0, 0, 0, 0)),
            pl.BlockSpec((3, N_HEADS, LANES, LANES), lambda b, i: (0, 0, 0, 0)),
        ],
        out_specs=pl.BlockSpec((1, tq, N_HEADS * HEAD_DIM), lambda b, i: (b, i, 0)),
        scratch_shapes=[
            pltpu.VMEM((l_pad, tq), jnp.float32),
            pltpu.VMEM((l_pad, tq), jnp.int16),
            pltpu.VMEM((l_pad, tq), jnp.int16),
            pltpu.VMEM((ck, N_HEADS * tq), jnp.float32),
            pltpu.VMEM((ck, N_HEADS * tq), jnp.float32),
            pltpu.VMEM((ck, 2 * LANES), MXU_DTYPE),
            pltpu.VMEM((ck, 2 * LANES), MXU_DTYPE),
            pltpu.VMEM((N_HEADS * tq, 2 * LANES), MXU_DTYPE),
            pltpu.VMEM((N_KV_HEADS, 1, GROUP * tq), jnp.float32),
            pltpu.VMEM((N_KV_HEADS, V_ROWS, GROUP * tq), jnp.float32),
        ],
        compiler_params=pltpu.CompilerParams(
            dimension_semantics=("arbitrary", "arbitrary"), vmem_limit_bytes=VMEM_LIMIT),
        name="dsa_attention",
    )(qi, q, wt, ki, kb, vt, nb)


def _layernorm(x, g, b):
    mu = jnp.mean(x, axis=-1, keepdims=True)
    xc = x - mu
    var = jnp.mean(xc * xc, axis=-1, keepdims=True)
    return xc * lax.rsqrt(var + LN_EPS) * g + b


def _mlp_kernel(x_ref, co_ref, at_ref, woa_ref, wob_ref, g1_ref, b1_ref, w1_ref, w2_ref, g2_ref, b2_ref,
                out_ref, *, ff_chunk):
    mix = _dot(co_ref[...], woa_ref[...]) + _dot(at_ref[...], wob_ref[...])
    x1 = _layernorm(ALPHA * x_ref[...] + mix, g1_ref[...], b1_ref[...])
    x1b = x1.astype(MXU_DTYPE)
    ff = jnp.zeros(x1.shape, jnp.float32)
    for c in range(D_FF // ff_chunk):
        hid = jnp.maximum(_dot(x1b, w1_ref[:, c * ff_chunk:(c + 1) * ff_chunk]), 0.0)
        ff = ff + _dot((hid * hid).astype(MXU_DTYPE), w2_ref[c * ff_chunk:(c + 1) * ff_chunk, :])
    out_ref[...] = _layernorm(ALPHA * x1 + ff, g2_ref[...], b2_ref[...])


def _mlp(x, conv_out, attn, woa, wob, g1, b1, w1, w2, g2, b2, *, tm, ff_chunk=1024):
    n_tok = x.shape[0]
    tok = lambda d: pl.BlockSpec((tm, d), lambda i: (i, 0))
    const = lambda shape: pl.BlockSpec(shape, lambda i: (0, 0), pipeline_mode=pl.Buffered(1))
    return pl.pallas_call(
        functools.partial(_mlp_kernel, ff_chunk=ff_chunk),
        out_shape=jax.ShapeDtypeStruct((n_tok, D_MODEL), jnp.float32),
        grid=(n_tok // tm,),
        in_specs=[
            tok(D_MODEL), tok(CONV_DIM), tok(N_HEADS * HEAD_DIM),
            const((CONV_DIM, D_MODEL)), const((N_HEADS * HEAD_DIM, D_MODEL)),
            const((1, D_MODEL)), const((1, D_MODEL)),
            const((D_MODEL, D_FF)), const((D_FF, D_MODEL)),
            const((1, D_MODEL)), const((1, D_MODEL)),
        ],
        out_specs=tok(D_MODEL),
        compiler_params=pltpu.CompilerParams(
            dimension_semantics=("arbitrary",), vmem_limit_bytes=VMEM_LIMIT),
        name="oproj_mlp",
    )(x, conv_out, attn, woa, wob, g1, b1, w1, w2, g2, b2)


def _tiles(seq):
    seq_pad = -(-seq // LANES) * LANES
    tm_in = KEY_CHUNK if seq_pad % KEY_CHUNK == 0 else LANES
    return seq_pad, tm_in


def _mlp_tile(n_tok):
    return 512 if n_tok % 512 == 0 else n_tok


def _layer(x, conv_prev, past, pos0, lw, nb):
    bsz, seq, _ = x.shape
    wa, wb, wc, conv_w, woa, wob, g1, b1, w1, w2, g2, b2 = lw
    seq_pad, tm_in = _tiles(seq)
    xp = x if seq_pad == seq else jnp.pad(x, ((0, 0), (0, seq_pad - seq), (0, 0)))
    (conv_out, q, kb, vt, k, v, qi, ki, kib, wt, new_conv) = _inproj(
        xp, conv_prev, wa, wb, wc, conv_w, tm=tm_in, n_valid=seq)
    k, v, ki = k[:, :seq], v[:, :seq], ki[:, :seq]
    ck = KEY_CHUNK
    if past is None:
        n_keys = seq
        assert seq_pad == seq and tm_in == ck
    else:
        pk, pv, pki = past
        n_keys = pk.shape[1] + seq
        l_pad = -(-(pos0 + seq_pad) // ck) * ck
        pad = l_pad - n_keys
        kb = jnp.pad(jnp.concatenate([pk.reshape(bsz, -1, N_KV_HEADS * HEAD_DIM), k], axis=1),
                     ((0, 0), (0, pad), (0, 0))).astype(MXU_DTYPE)
        v_all = jnp.pad(jnp.concatenate([pv.reshape(bsz, -1, N_KV_HEADS * HEAD_DIM), v], axis=1),
                        ((0, 0), (0, pad), (0, 0))).astype(MXU_DTYPE)
        kib = jnp.pad(jnp.concatenate([pki, ki], axis=1), ((0, 0), (0, pad), (0, 0))).astype(MXU_DTYPE)
        v_t =jnp.transpose(v_all.reshape(bsz, l_pad // ck, ck, N_KV_HEADS, HEAD_DIM), (0, 3, 1, 4, 2))
        ones = jnp.zeros((bsz, N_KV_HEADS, l_pad // ck, V_ROWS - HEAD_DIM, ck), MXU_DTYPE).at[:, :, :, 0].set(1)
        vt = jnp.concatenate([v_t, ones], axis=3)
    k_top = min(TOPK_MAX, n_keys // 4)
    attn = _attention(qi, q, wt, kib, kb, vt, nb, pos0=pos0, n_keys=n_keys, k_top=k_top)
    n_tok = bsz * seq
    y = _mlp(x.reshape(n_tok, D_MODEL), conv_out[:, :seq].reshape(n_tok, CONV_DIM),
             attn[:, :seq].reshape(n_tok, N_HEADS * HEAD_DIM), woa, wob, g1, b1, w1, w2, g2, b2,
             tm=_mlp_tile(n_tok))
    return (y.reshape(bsz, seq, D_MODEL),
            k.reshape(bsz, seq, N_KV_HEADS, HEAD_DIM), v.reshape(bsz, seq, N_KV_HEADS, HEAD_DIM), ki, new_conv)


def _layer_weights(l, w_in, conv_w, w_o, ln1_g, ln1_b, w_ff1, w_ff2, ln2_g, ln2_b):
    wi = w_in[l].astype(MXU_DTYPE)
    wa = wi[:, :3 * CONV_DIM]
    wb = wi[:, 3 * CONV_DIM:3 * CONV_DIM + D_QKV]
    wc = jnp.pad(wi[:, 3 * CONV_DIM + D_QKV:], ((0, 0), (0, D_IDX_PAD - D_IDX)))
    wo = w_o[l].astype(MXU_DTYPE)
    row = lambda a: a[l].reshape(1, D_MODEL)
    return (wa, wb, wc, conv_w[l], wo[:CONV_DIM], wo[CONV_DIM:], row(ln1_g), row(ln1_b),
            w_ff1[l].astype(MXU_DTYPE), w_ff2[l].astype(MXU_DTYPE), row(ln2_g), row(ln2_b))


def kernel(x_prompt, x_sample, cache_k, cache_v, cache_kidx, state_conv, w_in, conv_w, w_o, ln1_g, ln1_b,
           w_ff1, w_ff2, ln2_g, ln2_b, rel_bias):
    bp = x_prompt.shape[0]
    past_len = cache_k.shape[2]
    nb = _bias_tiles(rel_bias)
    hp, hs = x_prompt, x_sample
    outs = [[] for _ in range(8)]
    for l in range(DEPTH):
        lw = _layer_weights(l, w_in, conv_w, w_o, ln1_g, ln1_b, w_ff1, w_ff2, ln2_g, ln2_b)
        zero_conv = jnp.zeros((bp, CONV_WIDTH - 1, CONV_DIM), jnp.float32)
        hp, k_n, v_n, ki_n, c_n = _layer(hp, zero_conv, None, 0, lw, nb)
        for o, a in zip(outs[:4], (k_n, v_n, ki_n, c_n)):
            o.append(a)
        hs, k_n, v_n, ki_n, c_n = _layer(hs, state_conv[l], (cache_k[l], cache_v[l], cache_kidx[l]),
                                         past_len, lw, nb)
        for o, a in zip(outs[4:], (k_n, v_n, ki_n, c_n)):
            o.append(a)
    return (hp, hs) + tuple(jnp.stack(o) for o in outs)
```

```python
import functools
import math

import jax
import jax.numpy as jnp
from jax import lax
from jax.experimental import pallas as pl
from jax.experimental.pallas import tpu as pltpu

D_MODEL = 1024
CHUNK = 64
CONV_DIM = 512
CONV_WIDTH = 3
N_HEADS = 8
HEAD_DIM = 64
N_KV_HEADS = 2
GROUP = N_HEADS // N_KV_HEADS
N_IDX_HEADS = 8
IDX_DIM = 64
TOPK_MAX = 256
N_BUCKETS = 32
REL_MAX_DIST = 128
D_FF = 4 * D_MODEL
DEPTH = 2
ALPHA = (2 * DEPTH) ** 0.25
LN_EPS = 1e-5
NEG = -1e30
LOG2_E = math.log2(math.e)

LANES = 128
KEY_CHUNK = 512
V_ROWS = HEAD_DIM + 16
SENTINEL = float(jnp.finfo(jnp.float32).min)
M_INIT = -5e29
N_COUNTERS = 4

MXU_DTYPE = jnp.bfloat16
VMEM_LIMIT = 56 * 1024 * 1024

D_QKV = N_HEADS * HEAD_DIM + 2 * N_KV_HEADS * HEAD_DIM
D_IDX = N_IDX_HEADS * IDX_DIM + IDX_DIM + N_IDX_HEADS
D_IDX_PAD = 640


def _dot(a, b):
    return jnp.dot(a, b, preferred_element_type=jnp.float32)


def _dot_nt(a, b):
    return lax.dot_general(a, b, (((1,), (1,)), ((), ())), preferred_element_type=jnp.float32)


def _t5_bucket(rel):
    nb = N_BUCKETS // 2
    ret = (rel > 0).astype(jnp.int32) * nb
    n = jnp.abs(rel)
    max_exact = nb // 2
    nf = jnp.maximum(n, 1).astype(jnp.float32)
    large = max_exact + (jnp.log(nf / max_exact) / math.log(REL_MAX_DIST / max_exact)
                         * (nb - max_exact)).astype(jnp.int32)
    large = jnp.minimum(large, nb - 1)
    return ret + jnp.where(n < max_exact, n, large)


def _bias_kernel(rb_ref, out_ref):
    j = lax.broadcasted_iota(jnp.int32, (LANES, LANES), 0)
    i = lax.broadcasted_iota(jnp.int32, (LANES, LANES), 1)
    far_bucket = _t5_bucket(jnp.full((LANES, LANES), -(REL_MAX_DIST + 1), jnp.int32))
    buckets = [far_bucket, _t5_bucket(j - LANES - i), _t5_bucket(j - i)]
    for h in range(N_HEADS):
        vals = []
        for bk in buckets:
            v = jnp.zeros((LANES, LANES), jnp.float32)
            for b in range(N_BUCKETS):
                v = jnp.where(bk == b, rb_ref[b, h], v)
            vals.append(v)
        out_ref[0, h] = jnp.zeros((LANES, LANES), jnp.float32)
        out_ref[1, h] = (vals[1] - vals[0]) * LOG2_E
        out_ref[2, h] = (vals[2] - vals[0]) * LOG2_E


def _bias_tiles(rel_bias):
    return pl.pallas_call(
        _bias_kernel,
        out_shape=jax.ShapeDtypeStruct((3, N_HEADS, LANES, LANES), jnp.float32),
        in_specs=[pl.BlockSpec(memory_space=pltpu.SMEM)],
        name="rel_bias_tiles",
    )(rel_bias)


def _inproj_kernel(x_ref, cprev_ref, wa_ref, wb_ref, wc_ref, cw_ref,
                   co_ref, q_ref, kb_ref, vt_ref, k_ref, v_ref, qi_ref, ki_ref, kib_ref, wt_ref, nc_ref,
                   ubuf, *, tm, n_t, last_row):
    t = pl.program_id(1)
    xb = x_ref[0].astype(MXU_DTYPE)

    pa = _dot(xb, wa_ref[...])
    gb = pa[:, :CONV_DIM]
    u = pa[:, CONV_DIM:2 * CONV_DIM] * pa[:, 2 * CONV_DIM:]

    @pl.when(t == 0)
    def _():
        ubuf[0:8, :] = jnp.zeros((8, CONV_DIM), jnp.float32)
        ubuf[6:8, :] = cprev_ref[0]

    ubuf[8:8 + tm, :] = u
    cw = cw_ref[...]
    y = cw[0:1, :] * ubuf[6:6 + tm, :] + cw[1:2, :] * ubuf[7:7 + tm, :]
    y = y + cw[2:3, :] * u
    co_ref[0] = (gb * y).astype(co_ref.dtype)

    @pl.when(t == n_t - 1)
    def _():
        nc_ref[0] = ubuf[8 + last_row - 1:8 + last_row + 1, :]

    ubuf[0:8, :] = ubuf[tm:tm + 8, :]

    pb = _dot(xb, wb_ref[...])
    q = pb[:, :N_HEADS * HEAD_DIM] * (HEAD_DIM ** -0.5 * LOG2_E)
    zero_half = jnp.zeros((tm, HEAD_DIM), q_ref.dtype)
    for h in range(N_HEADS):
        qh = q[:, h * HEAD_DIM:(h + 1) * HEAD_DIM].astype(q_ref.dtype)
        halves = [zero_half] * N_KV_HEADS
        halves[h // GROUP] = qh
        q_ref[0, h] = jnp.concatenate(halves, axis=1)
    k = pb[:, N_HEADS * HEAD_DIM:N_HEADS * HEAD_DIM + N_KV_HEADS * HEAD_DIM]
    v = pb[:, N_HEADS * HEAD_DIM + N_KV_HEADS * HEAD_DIM:]
    k_ref[0] = k
    v_ref[0] = v
    kb_ref[0] = k.astype(kb_ref.dtype)
    vt = v.T
    ones_rows = (lax.broadcasted_iota(jnp.int32, (V_ROWS - HEAD_DIM, tm), 0) == 0).astype(vt_ref.dtype)
    for g in range(N_KV_HEADS):
        vt_ref[0, g, 0, 0:HEAD_DIM, :] = vt[g * HEAD_DIM:(g + 1) * HEAD_DIM].astype(vt_ref.dtype)
        vt_ref[0, g, 0, HEAD_DIM:V_ROWS, :] = ones_rows

    pc = _dot(xb, wc_ref[...])
    for h in range(N_IDX_HEADS):
        qi_ref[0, h] = pc[:, h * IDX_DIM:(h + 1) * IDX_DIM].astype(qi_ref.dtype)
    ki = pc[:, N_IDX_HEADS * IDX_DIM:N_IDX_HEADS * IDX_DIM + IDX_DIM]
    ki_ref[0] = ki
    kib_ref[0] = ki.astype(kib_ref.dtype)
    tail_t = pc[:, N_IDX_HEADS * IDX_DIM:].T
    wt_ref[0] = tail_t[IDX_DIM:IDX_DIM + N_IDX_HEADS] * (N_IDX_HEADS ** -0.5) * (IDX_DIM ** -0.5)


def _inproj(x, conv_prev, wa, wb, wc, conv_w, *, tm, n_valid):
    bsz, seq, _ = x.shape
    n_t = seq // tm
    f32 = jnp.float32
    const = lambda b, t: (0, 0)
    out_shape = (
        jax.ShapeDtypeStruct((bsz, seq, CONV_DIM), MXU_DTYPE),
        jax.ShapeDtypeStruct((bsz, N_HEADS, seq, N_KV_HEADS * HEAD_DIM), MXU_DTYPE),
        jax.ShapeDtypeStruct((bsz, seq, N_KV_HEADS * HEAD_DIM), MXU_DTYPE),
        jax.ShapeDtypeStruct((bsz, N_KV_HEADS, n_t, V_ROWS, tm), MXU_DTYPE),
        jax.ShapeDtypeStruct((bsz, seq, N_KV_HEADS * HEAD_DIM), f32),
        jax.ShapeDtypeStruct((bsz, seq, N_KV_HEADS * HEAD_DIM), f32),
        jax.ShapeDtypeStruct((bsz, N_IDX_HEADS, seq, IDX_DIM), MXU_DTYPE),
        jax.ShapeDtypeStruct((bsz, seq, IDX_DIM), f32),
        jax.ShapeDtypeStruct((bsz, seq, IDX_DIM), MXU_DTYPE),
        jax.ShapeDtypeStruct((bsz, N_IDX_HEADS, seq), f32),
        jax.ShapeDtypeStruct((bsz, CONV_WIDTH - 1, CONV_DIM), f32),
    )
    tok = lambda d: pl.BlockSpec((1, tm, d), lambda b, t: (b, t, 0))
    hm = lambda n, d: pl.BlockSpec((1, n, tm, d), lambda b, t: (b, 0, t, 0))
    state = pl.BlockSpec((1, CONV_WIDTH - 1, CONV_DIM), lambda b, t: (b, 0, 0))
    return pl.pallas_call(
        functools.partial(_inproj_kernel, tm=tm, n_t=n_t, last_row=n_valid - 1 - (n_t - 1) * tm),
        out_shape=out_shape,
        grid=(bsz, n_t),
        in_specs=[
            tok(D_MODEL), state,
            pl.BlockSpec((D_MODEL, 3 * CONV_DIM), const),
            pl.BlockSpec((D_MODEL, D_QKV), const),
            pl.BlockSpec((D_MODEL, D_IDX_PAD), const),
            pl.BlockSpec((CONV_WIDTH, CONV_DIM), const),
        ],
        out_specs=(
            tok(CONV_DIM), hm(N_HEADS, N_KV_HEADS * HEAD_DIM), tok(N_KV_HEADS * HEAD_DIM),
            pl.BlockSpec((1, N_KV_HEADS, 1, V_ROWS, tm), lambda b, t: (b, 0, t, 0, 0)),
            tok(N_KV_HEADS * HEAD_DIM), tok(N_KV_HEADS * HEAD_DIM), hm(N_IDX_HEADS, IDX_DIM),
            tok(IDX_DIM), tok(IDX_DIM),
            pl.BlockSpec((1, N_IDX_HEADS, tm), lambda b, t: (b, 0, t)), state,
        ),
        scratch_shapes=[pltpu.VMEM((tm + 8, CONV_DIM), f32)],
        compiler_params=pltpu.CompilerParams(
            dimension_semantics=("arbitrary", "arbitrary"), vmem_limit_bytes=VMEM_LIMIT),
        name="inproj_conv",
    )(x, conv_prev, wa, wb, wc, conv_w)


def _ordered_key(x):
    bits = lax.bitcast_convert_type(x, jnp.int32)
    return bits ^ (lax.shift_right_arithmetic(bits, 31) & jnp.int32(0x7FFFFFFF))


def _attn_kernel(qi_ref, q_ref, wt_ref, ki_ref, kb_ref, vt_ref, nb_ref, out_ref,
                 sc_ref, hi_ref, lo_ref, mma_ref, mmb_ref, lhsa_ref, lhsb_ref, rhs_ref, m_ref, acc_ref, *,
                 pos0, n_keys, k_top):
    tq, ck = LANES, KEY_CHUNK
    p0 = pos0 + pl.program_id(1) * tq
    n_blk = p0 // LANES + 1
    n_chunk = (n_blk * LANES + ck - 1) // ck
    q_pos = p0 + lax.broadcasted_iota(jnp.int32, (1, tq), 1)
    limit = jnp.minimum((q_pos // CHUNK + 1) * CHUNK, n_keys)
    row = lax.broadcasted_iota(jnp.int32, (ck, tq), 0)

    wt = wt_ref[0]
    qi_all = qi_ref[0].reshape(N_IDX_HEADS * tq, IDX_DIM)

    last = n_chunk - 1
    n_pair = (n_chunk + 1) // 2

    def store_heads(buf, res):
        for h in range(N_HEADS):
            buf[h] = res[:, h * tq:(h + 1) * tq]

    def idx_matmul(c, buf):
        s0 = pl.multiple_of(c * ck, ck)
        store_heads(buf, _dot_nt(ki_ref[0, pl.ds(s0, ck), :], qi_all))

    def idx_scores(c, buf):
        s0 = pl.multiple_of(c * ck, ck)
        acc = jnp.zeros((ck, tq), jnp.float32)
        for h in range(N_IDX_HEADS):
            acc = acc + jnp.maximum(buf[h], 0.0) * wt[h:h + 1, :]
        score = jnp.where(s0 + row < limit, acc, SENTINEL)
        sc_ref[pl.ds(s0, ck), :] = score
        key = _ordered_key(score)
        hi_ref[pl.ds(s0, ck), :] = lax.shift_right_arithmetic(key, 16).astype(jnp.int16)
        lo_ref[pl.ds(s0, ck), :] = ((key & jnp.int32(0xFFFF)) - 32768).astype(jnp.int16)

    idx_matmul(0, mma_ref)

    def score_pair(p, carry):
        c0 = 2 * p
        c1 = jnp.minimum(c0 + 1, last)
        idx_matmul(c1, mmb_ref)
        idx_scores(c0, mma_ref)
        idx_matmul(jnp.minimum(c0 + 2, last), mma_ref)
        idx_scores(c1, mmb_ref)
        return carry

    lax.fori_loop(0, n_pair, score_pair, 0)

    def count_ge(ref, cand):
        thr = jnp.broadcast_to(cand, (16, tq)).astype(jnp.int16)

        def body(c, cnts):
            chunk = ref[pl.ds(pl.multiple_of(c * ck, ck), ck), :]
            cnts = list(cnts)
            for j in range(ck // 16):
                hit = jnp.where(chunk[j * 16:(j + 1) * 16] >= thr, jnp.int16(1), jnp.int16(0))
                cnts[j % N_COUNTERS] = cnts[j % N_COUNTERS] + hit
            return tuple(cnts)
        zero = jnp.zeros((16, tq), jnp.int16)
        cnts = lax.fori_loop(0, n_chunk, body, (zero,) * N_COUNTERS)
        cnt = sum(c.astype(jnp.int32) for c in cnts)
        return jnp.sum(cnt, axis=0, keepdims=True)

    def bisect(ref, cnt_all):
        def step(it, carry):
            off, cnt_at = carry
            cand = off | lax.shift_left(jnp.int32(1), 15 - it)
            cnt = count_ge(ref, cand - 32768)
            ok = cnt >= k_top
            return jnp.where(ok, cand, off), jnp.where(ok, cnt, cnt_at)
        off, cnt_at = lax.fori_loop(0, 16, step, (jnp.zeros((1, tq), jnp.int32), cnt_all))
        return off - 32768, cnt_at

    cnt_all = jnp.full((1, tq), 1, jnp.int32) * (n_chunk * ck)
    t_hi, cnt_hi = bisect(hi_ref, cnt_all)

    def pin_chunk(c, carry):
        sl = pl.ds(pl.multiple_of(c * ck, ck), ck)
        hi = hi_ref[sl, :]
        lo_ref[sl, :] = jnp.where(hi > t_hi[0:1].astype(jnp.int16), jnp.int16(32767),
                                  jnp.where(hi < t_hi[0:1].astype(jnp.int16), jnp.int16(-32768), lo_ref[sl, :]))
        return carry

    lax.fori_loop(0, n_chunk, pin_chunk, 0)
    t_lo, cnt_ge = bisect(lo_ref, cnt_hi)
    thr_key = lax.shift_left(t_hi, 16) | (t_lo + 32768)
    thr = lax.bitcast_convert_type(
        thr_key ^ (lax.shift_right_arithmetic(thr_key, 31) & jnp.int32(0x7FFFFFFF)), jnp.float32)
    sent_next = jnp.nextafter(jnp.float32(SENTINEL), jnp.float32(0.0))
    has_k = thr > SENTINEL
    thr = jnp.maximum(thr, sent_next)

    excess = jnp.where(has_k & (cnt_ge > k_top), 1, 0)

    @pl.when(jnp.sum(excess) > 0)
    def _():
        def count_gt(c, cnt):
            s0 = pl.multiple_of(c * ck, ck)
            return cnt + jnp.sum(jnp.where(sc_ref[pl.ds(s0, ck), :] > thr, 1, 0), axis=0, keepdims=True)
        need = k_top - lax.fori_loop(0, n_chunk, count_gt, jnp.zeros((1, tq), jnp.int32))

        def ties_before(bound):
            def body(c, cnt):
                s0 = pl.multiple_of(c * ck, ck)
                tie = (sc_ref[pl.ds(s0, ck), :] == thr) & (s0 + row < bound)
                return cnt + jnp.sum(jnp.where(tie, 1, 0), axis=0, keepdims=True)
            return lax.fori_loop(0, n_chunk, body, jnp.zeros((1, tq), jnp.int32))

        n_bits = max(1, (n_keys + ck).bit_length())

        def cut_step(it, cut):
            cand = cut | lax.shift_left(jnp.int32(1), n_bits - 1 - it)
            return jnp.where(ties_before(cand) <= need, cand, cut)

        cut = lax.fori_loop(0, n_bits, cut_step, jnp.zeros((1, tq), jnp.int32))

        def drop(c, carry):
            s0 = pl.multiple_of(c * ck, ck)
            sc = sc_ref[pl.ds(s0, ck), :]
            sc_ref[pl.ds(s0, ck), :] = jnp.where((sc == thr) & (s0 + row >= cut), SENTINEL, sc)
            return carry
        lax.fori_loop(0, n_chunk, drop, 0)

    m_ref[...] = jnp.full(m_ref.shape, M_INIT, jnp.float32)
    acc_ref[...] = jnp.zeros(acc_ref.shape, jnp.float32)

    eye = (lax.broadcasted_iota(jnp.int32, (tq, tq), 0) == lax.broadcasted_iota(jnp.int32, (tq, tq), 1))
    for h in range(N_HEADS):
        rhs_ref[h * tq:(h + 1) * tq, 0:LANES] = eye.astype(MXU_DTYPE)
        rhs_ref[h * tq:(h + 1) * tq, LANES:2 * LANES] = q_ref[0, h]

    def qk_matmul(c, live, lhs, buf):
        s0 = pl.multiple_of(c * ck, ck)
        sel = (sc_ref[pl.ds(s0, ck), :] >= thr) & live
        lhs[:, 0:LANES] = jnp.where(sel, 0.0, NEG).astype(MXU_DTYPE)
        lhs[:, LANES:2 * LANES] = kb_ref[0, pl.ds(s0, ck), :]
        store_heads(buf, _dot_nt(lhs[...], rhs_ref[...]))

    def logits(buf, c, h, near):
        blk = buf[h]
        if near:
            bias = [nb_ref[jnp.clip(c * (ck // LANES) + j - (n_blk - 3), 0, 2), h] for j in range(ck // LANES)]
            blk = blk + jnp.concatenate(bias, axis=0)
        return blk

    def attend_chunk(c, c_next, live_next, buf, buf_next, lhs_next, near):
        qk_matmul(c_next, live_next, lhs_next, buf_next)
        for g in range(N_KV_HEADS):
            m_old = m_ref[g]
            m_new = jnp.concatenate(
                [jnp.maximum(m_old[:, r * tq:(r + 1) * tq],
                             jnp.max(logits(buf, c, g * GROUP + r, near), axis=0, keepdims=True))
                 for r in range(GROUP)], axis=1)
            scale = jnp.exp2(m_old - m_new)
            m_ref[g] = m_new
            probs = jnp.concatenate(
                [jnp.exp2(logits(buf, c, g * GROUP + r, near) - m_new[:, r * tq:(r + 1) * tq]).astype(MXU_DTYPE)
                 for r in range(GROUP)], axis=1)
            pv = _dot(vt_ref[0, g, c], probs)
            acc_ref[g] = scale * acc_ref[g] + pv

    qk_matmul(0, True, lhsa_ref, mma_ref)

    def attend_pair(p, near):
        c0 = 2 * p
        c1 = jnp.minimum(c0 + 1, last)
        attend_chunk(c0, c1, c0 + 1 <= last, mma_ref, mmb_ref, lhsb_ref, near)
        attend_chunk(c1, jnp.minimum(c0 + 2, last), True, mmb_ref, mma_ref, lhsa_ref, near)

    n_far = jnp.maximum(n_pair - 2, 0)

    def far_body(p, carry):
        attend_pair(p, near=False)
        return carry

    def near_body(p, carry):
        attend_pair(p, near=True)
        return carry

    lax.fori_loop(0, n_far, far_body, 0)
    lax.fori_loop(n_far, n_pair, near_body, 0)

    for g in range(N_KV_HEADS):
        acc = acc_ref[g]
        o = acc[0:HEAD_DIM] / acc[HEAD_DIM:HEAD_DIM + 1]
        for r in range(GROUP):
            h = g * GROUP + r
            out_ref[0, :, h * HEAD_DIM:(h + 1) * HEAD_DIM] = o[:, r * tq:(r + 1) * tq].T.astype(out_ref.dtype)


def _attention(qi, q, wt, ki, kb, vt, nb, *, pos0, n_keys, k_top):
    bsz, _, seq, _ = q.shape
    tq, ck = LANES, KEY_CHUNK
    l_pad = ki.shape[1]
    n_q = seq // tq
    assert pos0 % LANES == 0 and seq % tq == 0 and l_pad % ck == 0
    assert pos0 + seq <= l_pad and n_keys <= l_pad and k_top <= ck
    qspec = lambda n, d: pl.BlockSpec((1, n, tq, d), lambda b, i: (b, 0, i, 0))
    return pl.pallas_call(
        functools.partial(_attn_kernel, pos0=pos0, n_keys=n_keys, k_top=k_top),
        out_shape=jax.ShapeDtypeStruct((bsz, seq, N_HEADS * HEAD_DIM), MXU_DTYPE),
        grid=(bsz, n_q),
        in_specs=[
            qspec(N_IDX_HEADS, IDX_DIM), qspec(N_HEADS, N_KV_HEADS * HEAD_DIM),
            pl.BlockSpec((1, N_IDX_HEADS, tq), lambda b, i: (b, 0, i)),
            pl.BlockSpec((1, l_pad, IDX_DIM), lambda b, i: (b, 0, 0)),
            pl.BlockSpec((1, l_pad, N_KV_HEADS * HEAD_DIM), lambda b, i: (b, 0, 0)),
            pl.BlockSpec((1, N_KV_HEADS, l_pad // ck, V_ROWS, ck), lambda b, i: (b, 0, 0, 0, 0)),
            pl.BlockSpec((3, N_HEADS, LANES, LANES), lambda b, i: (0, 0, 0, 0)),
        ],
        out_specs=pl.BlockSpec((1, tq, N_HEADS * HEAD_DIM), lambda b, i: (b, i, 0)),
        scratch_shapes=[
            pltpu.VMEM((l_pad, tq), jnp.float32),
            pltpu.VMEM((l_pad, tq), jnp.int16),
            pltpu.VMEM((l_pad, tq), jnp.int16),
            pltpu.VMEM((N_HEADS, ck, tq), jnp.float32),
            pltpu.VMEM((N_HEADS, ck, tq), jnp.float32),
            pltpu.VMEM((ck, 2 * LANES), MXU_DTYPE),
            pltpu.VMEM((ck, 2 * LANES), MXU_DTYPE),
            pltpu.VMEM((N_HEADS * tq, 2 * LANES), MXU_DTYPE),
            pltpu.VMEM((N_KV_HEADS, 1, GROUP * tq), jnp.float32),
            pltpu.VMEM((N_KV_HEADS, V_ROWS, GROUP * tq), jnp.float32),
        ],
        compiler_params=pltpu.CompilerParams(
            dimension_semantics=("arbitrary", "arbitrary"), vmem_limit_bytes=VMEM_LIMIT),
        name="dsa_attention",
    )(qi, q, wt, ki, kb, vt, nb)


def _layernorm(x, g, b):
    mu = jnp.mean(x, axis=-1, keepdims=True)
    xc = x - mu
    var = jnp.mean(xc * xc, axis=-1, keepdims=True)
    return xc * lax.rsqrt(var + LN_EPS) * g + b


def _mlp_kernel(x_ref, co_ref, at_ref, woa_ref, wob_ref, g1_ref, b1_ref, w1_ref, w2_ref, g2_ref, b2_ref,
                out_ref, *, ff_chunk):
    mix = _dot(co_ref[...], woa_ref[...]) + _dot(at_ref[...], wob_ref[...])
    x1 = _layernorm(ALPHA * x_ref[...] + mix, g1_ref[...], b1_ref[...])
    x1b = x1.astype(MXU_DTYPE)
    ff = jnp.zeros(x1.shape, jnp.float32)
    for c in range(D_FF // ff_chunk):
        hid = jnp.maximum(_dot(x1b, w1_ref[:, c * ff_chunk:(c + 1) * ff_chunk]), 0.0)
        ff = ff + _dot((hid * hid).astype(MXU_DTYPE), w2_ref[c * ff_chunk:(c + 1) * ff_chunk, :])
    out_ref[...] = _layernorm(ALPHA * x1 + ff, g2_ref[...], b2_ref[...])


def _mlp(x, conv_out, attn, woa, wob, g1, b1, w1, w2, g2, b2, *, tm, ff_chunk=1024):
    n_tok = x.shape[0]
    tok = lambda d: pl.BlockSpec((tm, d), lambda i: (i, 0))
    const = lambda shape: pl.BlockSpec(shape, lambda i: (0, 0), pipeline_mode=pl.Buffered(1))
    return pl.pallas_call(
        functools.partial(_mlp_kernel, ff_chunk=ff_chunk),
        out_shape=jax.ShapeDtypeStruct((n_tok, D_MODEL), jnp.float32),
        grid=(n_tok // tm,),
        in_specs=[
            tok(D_MODEL), tok(CONV_DIM), tok(N_HEADS * HEAD_DIM),
            const((CONV_DIM, D_MODEL)), const((N_HEADS * HEAD_DIM, D_MODEL)),
            const((1, D_MODEL)), const((1, D_MODEL)),
            const((D_MODEL, D_FF)), const((D_FF, D_MODEL)),
            const((1, D_MODEL)), const((1, D_MODEL)),
        ],
        out_specs=tok(D_MODEL),
        compiler_params=pltpu.CompilerParams(
            dimension_semantics=("arbitrary",), vmem_limit_bytes=VMEM_LIMIT),
        name="oproj_mlp",
    )(x, conv_out, attn, woa, wob, g1, b1, w1, w2, g2, b2)


def _tiles(seq):
    seq_pad = -(-seq // LANES) * LANES
    tm_in = KEY_CHUNK if seq_pad % KEY_CHUNK == 0 else LANES
    return seq_pad, tm_in


def _mlp_tile(n_tok):
    return 512 if n_tok % 512 == 0 else n_tok


def _layer(x, conv_prev, past, pos0, lw, nb):
    bsz, seq, _ = x.shape
    wa, wb, wc, conv_w, woa, wob, g1, b1, w1, w2, g2, b2 = lw
    seq_pad, tm_in = _tiles(seq)
    xp = x if seq_pad == seq else jnp.pad(x, ((0, 0), (0, seq_pad - seq), (0, 0)))
    (conv_out, q, kb, vt, k, v, qi, ki, kib, wt, new_conv) = _inproj(
        xp, conv_prev, wa, wb, wc, conv_w, tm=tm_in, n_valid=seq)
    k, v, ki = k[:, :seq], v[:, :seq], ki[:, :seq]
    ck = KEY_CHUNK
    if past is None:
        n_keys = seq
        assert seq_pad == seq and tm_in == ck
    else:
        pk, pv, pki = past
        n_keys = pk.shape[1] + seq
        l_pad = -(-(pos0 + seq_pad) // ck) * ck
        pad = l_pad - n_keys
        kb = jnp.pad(jnp.concatenate([pk.reshape(bsz, -1, N_KV_HEADS * HEAD_DIM), k], axis=1),
                     ((0, 0), (0, pad), (0, 0))).astype(MXU_DTYPE)
        v_all = jnp.pad(jnp.concatenate([pv.reshape(bsz, -1, N_KV_HEADS * HEAD_DIM), v], axis=1),
                        ((0, 0), (0, pad), (0, 0))).astype(MXU_DTYPE)
        kib = jnp.pad(jnp.concatenate([pki, ki], axis=1), ((0, 0), (0, pad), (0, 0))).astype(MXU_DTYPE)
        v_t =jnp.transpose(v_all.reshape(bsz, l_pad // ck, ck, N_KV_HEADS, HEAD_DIM), (0, 3, 1, 4, 2))
        ones = jnp.zeros((bsz, N_KV_HEADS, l_pad // ck, V_ROWS - HEAD_DIM, ck), MXU_DTYPE).at[:, :, :, 0].set(1)
        vt = jnp.concatenate([v_t, ones], axis=3)
    k_top = min(TOPK_MAX, n_keys // 4)
    attn = _attention(qi, q, wt, kib, kb, vt, nb, pos0=pos0, n_keys=n_keys, k_top=k_top)
    n_tok = bsz * seq
    y = _mlp(x.reshape(n_tok, D_MODEL), conv_out[:, :seq].reshape(n_tok, CONV_DIM),
             attn[:, :seq].reshape(n_tok, N_HEADS * HEAD_DIM), woa, wob, g1, b1, w1, w2, g2, b2,
             tm=_mlp_tile(n_tok))
    return (y.reshape(bsz, seq, D_MODEL),
            k.reshape(bsz, seq, N_KV_HEADS, HEAD_DIM), v.reshape(bsz, seq, N_KV_HEADS, HEAD_DIM), ki, new_conv)


def _layer_weights(l, w_in, conv_w, w_o, ln1_g, ln1_b, w_ff1, w_ff2, ln2_g, ln2_b):
    wi = w_in[l].astype(MXU_DTYPE)
    wa = wi[:, :3 * CONV_DIM]
    wb = wi[:, 3 * CONV_DIM:3 * CONV_DIM + D_QKV]
    wc = jnp.pad(wi[:, 3 * CONV_DIM + D_QKV:], ((0, 0), (0, D_IDX_PAD - D_IDX)))
    wo = w_o[l].astype(MXU_DTYPE)
    row = lambda a: a[l].reshape(1, D_MODEL)
    return (wa, wb, wc, conv_w[l], wo[:CONV_DIM], wo[CONV_DIM:], row(ln1_g), row(ln1_b),
            w_ff1[l].astype(MXU_DTYPE), w_ff2[l].astype(MXU_DTYPE), row(ln2_g), row(ln2_b))


def kernel(x_prompt, x_sample, cache_k, cache_v, cache_kidx, state_conv, w_in, conv_w, w_o, ln1_g, ln1_b,
           w_ff1, w_ff2, ln2_g, ln2_b, rel_bias):
    bp = x_prompt.shape[0]
    past_len = cache_k.shape[2]
    nb = _bias_tiles(rel_bias)
    hp, hs = x_prompt, x_sample
    outs = [[] for _ in range(8)]
    for l in range(DEPTH):
        lw = _layer_weights(l, w_in, conv_w, w_o, ln1_g, ln1_b, w_ff1, w_ff2, ln2_g, ln2_b)
        zero_conv = jnp.zeros((bp, CONV_WIDTH - 1, CONV_DIM), jnp.float32)
        hp, k_n, v_n, ki_n, c_n = _layer(hp, zero_conv, None, 0, lw, nb)
        for o, a in zip(outs[:4], (k_n, v_n, ki_n, c_n)):
            o.append(a)
        hs, k_n, v_n, ki_n, c_n = _layer(hs, state_conv[l], (cache_k[l], cache_v[l], cache_kidx[l]),
                                         past_len, lw, nb)
        for o, a in zip(outs[4:], (k_n, v_n, ki_n, c_n)):
            o.append(a)
    return (hp, hs) + tuple(jnp.stack(o) for o in outs)
```

```python
import functools
import math

import jax
import jax.numpy as jnp
from jax import lax
from jax.experimental import pallas as pl
from jax.experimental.pallas import tpu as pltpu

D_MODEL = 1024
CHUNK = 64
CONV_DIM = 512
CONV_WIDTH = 3
N_HEADS = 8
HEAD_DIM = 64
N_KV_HEADS = 2
GROUP = N_HEADS // N_KV_HEADS
N_IDX_HEADS = 8
IDX_DIM = 64
TOPK_MAX = 256
N_BUCKETS = 32
REL_MAX_DIST = 128
D_FF = 4 * D_MODEL
DEPTH = 2
ALPHA = (2 * DEPTH) ** 0.25
LN_EPS = 1e-5
NEG = -1e30
LOG2_E = math.log2(math.e)

LANES = 128
KEY_CHUNK = 512
V_ROWS = HEAD_DIM + 16
SENTINEL = float(jnp.finfo(jnp.float32).min)
M_INIT = -5e29
N_COUNTERS = 4

MXU_DTYPE = jnp.bfloat16
VMEM_LIMIT = 56 * 1024 * 1024

D_QKV = N_HEADS * HEAD_DIM + 2 * N_KV_HEADS * HEAD_DIM
D_IDX = N_IDX_HEADS * IDX_DIM + IDX_DIM + N_IDX_HEADS
D_IDX_PAD = 640


def _dot(a, b):
    return jnp.dot(a, b, preferred_element_type=jnp.float32)


def _dot_nt(a, b):
    return lax.dot_general(a, b, (((1,), (1,)), ((), ())), preferred_element_type=jnp.float32)


def _t5_bucket(rel):
    nb = N_BUCKETS // 2
    ret = (rel > 0).astype(jnp.int32) * nb
    n = jnp.abs(rel)
    max_exact = nb // 2
    nf = jnp.maximum(n, 1).astype(jnp.float32)
    large = max_exact + (jnp.log(nf / max_exact) / math.log(REL_MAX_DIST / max_exact)
                         * (nb - max_exact)).astype(jnp.int32)
    large = jnp.minimum(large, nb - 1)
    return ret + jnp.where(n < max_exact, n, large)


def _bias_kernel(rb_ref, out_ref):
    j = lax.broadcasted_iota(jnp.int32, (LANES, LANES), 0)
    i = lax.broadcasted_iota(jnp.int32, (LANES, LANES), 1)
    far_bucket = _t5_bucket(jnp.full((LANES, LANES), -(REL_MAX_DIST + 1), jnp.int32))
    buckets = [far_bucket, _t5_bucket(j - LANES - i), _t5_bucket(j - i)]
    for h in range(N_HEADS):
        vals = []
        for bk in buckets:
            v = jnp.zeros((LANES, LANES), jnp.float32)
            for b in range(N_BUCKETS):
                v = jnp.where(bk == b, rb_ref[b, h], v)
            vals.append(v)
        out_ref[0, h] = jnp.zeros((LANES, LANES), jnp.float32)
        out_ref[1, h] = (vals[1] - vals[0]) * LOG2_E
        out_ref[2, h] = (vals[2] - vals[0]) * LOG2_E


def _bias_tiles(rel_bias):
    return pl.pallas_call(
        _bias_kernel,
        out_shape=jax.ShapeDtypeStruct((3, N_HEADS, LANES, LANES), jnp.float32),
        in_specs=[pl.BlockSpec(memory_space=pltpu.SMEM)],
        name="rel_bias_tiles",
    )(rel_bias)


def _inproj_kernel(x_ref, cprev_ref, wa_ref, wb_ref, wc_ref, cw_ref,
                   co_ref, q_ref, kb_ref, vt_ref, k_ref, v_ref, qi_ref, ki_ref, kib_ref, wt_ref, nc_ref,
                   ubuf, *, tm, n_t, last_row):
    t = pl.program_id(1)
    xb = x_ref[0].astype(MXU_DTYPE)

    pa = _dot(xb, wa_ref[...])
    gb = pa[:, :CONV_DIM]
    u = pa[:, CONV_DIM:2 * CONV_DIM] * pa[:, 2 * CONV_DIM:]

    @pl.when(t == 0)
    def _():
        ubuf[0:8, :] = jnp.zeros((8, CONV_DIM), jnp.float32)
        ubuf[6:8, :] = cprev_ref[0]

    ubuf[8:8 + tm, :] = u
    cw = cw_ref[...]
    y = cw[0:1, :] * ubuf[6:6 + tm, :] + cw[1:2, :] * ubuf[7:7 + tm, :]
    y = y + cw[2:3, :] * u
    co_ref[0] = (gb * y).astype(co_ref.dtype)

    @pl.when(t == n_t - 1)
    def _():
        nc_ref[0] = ubuf[8 + last_row - 1:8 + last_row + 1, :]

    ubuf[0:8, :] = ubuf[tm:tm + 8, :]

    pb = _dot(xb, wb_ref[...])
    q = pb[:, :N_HEADS * HEAD_DIM] * (HEAD_DIM ** -0.5 * LOG2_E)
    zero_half = jnp.zeros((tm, HEAD_DIM), q_ref.dtype)
    for h in range(N_HEADS):
        qh = q[:, h * HEAD_DIM:(h + 1) * HEAD_DIM].astype(q_ref.dtype)
        halves = [zero_half] * N_KV_HEADS
        halves[h // GROUP] = qh
        q_ref[0, h] = jnp.concatenate(halves, axis=1)
    k = pb[:, N_HEADS * HEAD_DIM:N_HEADS * HEAD_DIM + N_KV_HEADS * HEAD_DIM]
    v = pb[:, N_HEADS * HEAD_DIM + N_KV_HEADS * HEAD_DIM:]
    k_ref[0] = k
    v_ref[0] = v
    kb_ref[0] = k.astype(kb_ref.dtype)
    vt = v.T
    ones_rows = (lax.broadcasted_iota(jnp.int32, (V_ROWS - HEAD_DIM, tm), 0) == 0).astype(vt_ref.dtype)
    for g in range(N_KV_HEADS):
        vt_ref[0, g, 0, 0:HEAD_DIM, :] = vt[g * HEAD_DIM:(g + 1) * HEAD_DIM].astype(vt_ref.dtype)
        vt_ref[0, g, 0, HEAD_DIM:V_ROWS, :] = ones_rows

    pc = _dot(xb, wc_ref[...])
    for h in range(N_IDX_HEADS):
        qi_ref[0, h] = pc[:, h * IDX_DIM:(h + 1) * IDX_DIM].astype(qi_ref.dtype)
    ki = pc[:, N_IDX_HEADS * IDX_DIM:N_IDX_HEADS * IDX_DIM + IDX_DIM]
    ki_ref[0] = ki
    kib_ref[0] = ki.astype(kib_ref.dtype)
    tail_t = pc[:, N_IDX_HEADS * IDX_DIM:].T
    wt_ref[0] = tail_t[IDX_DIM:IDX_DIM + N_IDX_HEADS] * (N_IDX_HEADS ** -0.5) * (IDX_DIM ** -0.5)


def _inproj(x, conv_prev, wa, wb, wc, conv_w, *, tm, n_valid):
    bsz, seq, _ = x.shape
    n_t = seq // tm
    f32 = jnp.float32
    const = lambda b, t: (0, 0)
    out_shape = (
        jax.ShapeDtypeStruct((bsz, seq, CONV_DIM), MXU_DTYPE),
        jax.ShapeDtypeStruct((bsz, N_HEADS, seq, N_KV_HEADS * HEAD_DIM), MXU_DTYPE),
        jax.ShapeDtypeStruct((bsz, seq, N_KV_HEADS * HEAD_DIM), MXU_DTYPE),
        jax.ShapeDtypeStruct((bsz, N_KV_HEADS, n_t, V_ROWS, tm), MXU_DTYPE),
        jax.ShapeDtypeStruct((bsz, seq, N_KV_HEADS * HEAD_DIM), f32),
        jax.ShapeDtypeStruct((bsz, seq, N_KV_HEADS * HEAD_DIM), f32),
        jax.ShapeDtypeStruct((bsz, N_IDX_HEADS, seq, IDX_DIM), MXU_DTYPE),
        jax.ShapeDtypeStruct((bsz, seq, IDX_DIM), f32),
        jax.ShapeDtypeStruct((bsz, seq, IDX_DIM), MXU_DTYPE),
        jax.ShapeDtypeStruct((bsz, N_IDX_HEADS, seq), f32),
        jax.ShapeDtypeStruct((bsz, CONV_WIDTH - 1, CONV_DIM), f32),
    )
    tok = lambda d: pl.BlockSpec((1, tm, d), lambda b, t: (b, t, 0))
    hm = lambda n, d: pl.BlockSpec((1, n, tm, d), lambda b, t: (b, 0, t, 0))
    state = pl.BlockSpec((1, CONV_WIDTH - 1, CONV_DIM), lambda b, t: (b, 0, 0))
    return pl.pallas_call(
        functools.partial(_inproj_kernel, tm=tm, n_t=n_t, last_row=n_valid - 1 - (n_t - 1) * tm),
        out_shape=out_shape,
        grid=(bsz, n_t),
        in_specs=[
            tok(D_MODEL), state,
            pl.BlockSpec((D_MODEL, 3 * CONV_DIM), const),
            pl.BlockSpec((D_MODEL, D_QKV), const),
            pl.BlockSpec((D_MODEL, D_IDX_PAD), const),
            pl.BlockSpec((CONV_WIDTH, CONV_DIM), const),
        ],
        out_specs=(
            tok(CONV_DIM), hm(N_HEADS, N_KV_HEADS * HEAD_DIM), tok(N_KV_HEADS * HEAD_DIM),
            pl.BlockSpec((1, N_KV_HEADS, 1, V_ROWS, tm), lambda b, t: (b, 0, t, 0, 0)),
            tok(N_KV_HEADS * HEAD_DIM), tok(N_KV_HEADS * HEAD_DIM), hm(N_IDX_HEADS, IDX_DIM),
            tok(IDX_DIM), tok(IDX_DIM),
            pl.BlockSpec((1, N_IDX_HEADS, tm), lambda b, t: (b, 0, t)), state,
        ),
        scratch_shapes=[pltpu.VMEM((tm + 8, CONV_DIM), f32)],
        compiler_params=pltpu.CompilerParams(
            dimension_semantics=("arbitrary", "arbitrary"), vmem_limit_bytes=VMEM_LIMIT),
        name="inproj_conv",
    )(x, conv_prev, wa, wb, wc, conv_w)


def _ordered_key(x):
    bits = lax.bitcast_convert_type(x, jnp.int32)
    return bits ^ (lax.shift_right_arithmetic(bits, 31) & jnp.int32(0x7FFFFFFF))


def _attn_kernel(qi_ref, q_ref, wt_ref, ki_ref, kb_ref, vt_ref, nb_ref, out_ref,
                 sc_ref, hi_ref, lo_ref, mma_ref, mmb_ref, lhsa_ref, lhsb_ref, rhs_ref, m_ref, acc_ref, *,
                 pos0, n_keys, k_top):
    tq, ck = LANES, KEY_CHUNK
    p0 = pos0 + pl.program_id(1) * tq
    n_blk = p0 // LANES + 1
    n_chunk = (n_blk * LANES + ck - 1) // ck
    q_pos = p0 + lax.broadcasted_iota(jnp.int32, (1, tq), 1)
    limit = jnp.minimum((q_pos // CHUNK + 1) * CHUNK, n_keys)
    row = lax.broadcasted_iota(jnp.int32, (ck, tq), 0)

    wt = wt_ref[0]
    qi_all = qi_ref[0].reshape(N_IDX_HEADS * tq, IDX_DIM)

    last = n_chunk - 1
    n_pair = n_chunk // 2
    has_tail = n_chunk % 2 == 1

    def store_heads(buf, res):
        for h in range(N_HEADS):
            buf[h] = res[:, h * tq:(h + 1) * tq]

    def idx_matmul(c, buf):
        s0 = pl.multiple_of(c * ck, ck)
        store_heads(buf, _dot_nt(ki_ref[0, pl.ds(s0, ck), :], qi_all))

    def idx_scores(c, buf):
        s0 = pl.multiple_of(c * ck, ck)
        acc = jnp.zeros((ck, tq), jnp.float32)
        for h in range(N_IDX_HEADS):
            acc = acc + jnp.maximum(buf[h], 0.0) * wt[h:h + 1, :]
        score = jnp.where(s0 + row < limit, acc, SENTINEL)
        sc_ref[pl.ds(s0, ck), :] = score
        key = _ordered_key(score)
        hi_ref[pl.ds(s0, ck), :] = lax.shift_right_arithmetic(key, 16).astype(jnp.int16)
        lo_ref[pl.ds(s0, ck), :] = ((key & jnp.int32(0xFFFF)) - 32768).astype(jnp.int16)

    idx_matmul(0, mma_ref)

    def score_pair(p, carry):
        c0 = 2 * p
        idx_matmul(c0 + 1, mmb_ref)
        idx_scores(c0, mma_ref)
        idx_matmul(jnp.minimum(c0 + 2, last), mma_ref)
        idx_scores(c0 + 1, mmb_ref)
        return carry

    lax.fori_loop(0, n_pair, score_pair, 0)

    @pl.when(has_tail)
    def _():
        idx_scores(last, mma_ref)

    def count_ge(ref, cand):
        thr = jnp.broadcast_to(cand, (16, tq)).astype(jnp.int16)

        def body(c, cnts):
            chunk = ref[pl.ds(pl.multiple_of(c * ck, ck), ck), :]
            cnts = list(cnts)
            for j in range(ck // 16):
                hit = jnp.where(chunk[j * 16:(j + 1) * 16] >= thr, jnp.int16(1), jnp.int16(0))
                cnts[j % N_COUNTERS] = cnts[j % N_COUNTERS] + hit
            return tuple(cnts)
        zero = jnp.zeros((16, tq), jnp.int16)
        cnts = lax.fori_loop(0, n_chunk, body, (zero,) * N_COUNTERS)
        cnt = sum(c.astype(jnp.int32) for c in cnts)
        return jnp.sum(cnt, axis=0, keepdims=True)

    def bisect(ref, cnt_all):
        def step(it, carry):
            off, cnt_at = carry
            cand = off | lax.shift_left(jnp.int32(1), 15 - it)
            cnt = count_ge(ref, cand - 32768)
            ok = cnt >= k_top
            return jnp.where(ok, cand, off), jnp.where(ok, cnt, cnt_at)
        off, cnt_at = lax.fori_loop(0, 16, step, (jnp.zeros((1, tq), jnp.int32), cnt_all))
        return off - 32768, cnt_at

    cnt_all = jnp.full((1, tq), 1, jnp.int32) * (n_chunk * ck)
    t_hi, cnt_hi = bisect(hi_ref, cnt_all)

    def pin_chunk(c, carry):
        sl = pl.ds(pl.multiple_of(c * ck, ck), ck)
        hi = hi_ref[sl, :]
        lo_ref[sl, :] = jnp.where(hi > t_hi[0:1].astype(jnp.int16), jnp.int16(32767),
                                  jnp.where(hi < t_hi[0:1].astype(jnp.int16), jnp.int16(-32768), lo_ref[sl, :]))
        return carry

    lax.fori_loop(0, n_chunk, pin_chunk, 0)
    t_lo, cnt_ge = bisect(lo_ref, cnt_hi)
    thr_key = lax.shift_left(t_hi, 16) | (t_lo + 32768)
    thr = lax.bitcast_convert_type(
        thr_key ^ (lax.shift_right_arithmetic(thr_key, 31) & jnp.int32(0x7FFFFFFF)), jnp.float32)
    sent_next = jnp.nextafter(jnp.float32(SENTINEL), jnp.float32(0.0))
    has_k = thr > SENTINEL
    thr = jnp.maximum(thr, sent_next)

    excess = jnp.where(has_k & (cnt_ge > k_top), 1, 0)

    @pl.when(jnp.sum(excess) > 0)
    def _():
        def count_gt(c, cnt):
            s0 = pl.multiple_of(c * ck, ck)
            return cnt + jnp.sum(jnp.where(sc_ref[pl.ds(s0, ck), :] > thr, 1, 0), axis=0, keepdims=True)
        need = k_top - lax.fori_loop(0, n_chunk, count_gt, jnp.zeros((1, tq), jnp.int32))

        def ties_before(bound):
            def body(c, cnt):
                s0 = pl.multiple_of(c * ck, ck)
                tie = (sc_ref[pl.ds(s0, ck), :] == thr) & (s0 + row < bound)
                return cnt + jnp.sum(jnp.where(tie, 1, 0), axis=0, keepdims=True)
            return lax.fori_loop(0, n_chunk, body, jnp.zeros((1, tq), jnp.int32))

        n_bits = max(1, (n_keys + ck).bit_length())

        def cut_step(it, cut):
            cand = cut | lax.shift_left(jnp.int32(1), n_bits - 1 - it)
            return jnp.where(ties_before(cand) <= need, cand, cut)

        cut = lax.fori_loop(0, n_bits, cut_step, jnp.zeros((1, tq), jnp.int32))

        def drop(c, carry):
            s0 = pl.multiple_of(c * ck, ck)
            sc = sc_ref[pl.ds(s0, ck), :]
            sc_ref[pl.ds(s0, ck), :] = jnp.where((sc == thr) & (s0 + row >= cut), SENTINEL, sc)
            return carry
        lax.fori_loop(0, n_chunk, drop, 0)

    m_ref[...] = jnp.full(m_ref.shape, M_INIT, jnp.float32)
    acc_ref[...] = jnp.zeros(acc_ref.shape, jnp.float32)

    eye = (lax.broadcasted_iota(jnp.int32, (tq, tq), 0) == lax.broadcasted_iota(jnp.int32, (tq, tq), 1))
    for h in range(N_HEADS):
        rhs_ref[h * tq:(h + 1) * tq, 0:LANES] = eye.astype(MXU_DTYPE)
        rhs_ref[h * tq:(h + 1) * tq, LANES:2 * LANES] = q_ref[0, h]

    def qk_matmul(c, live, lhs, buf):
        s0 = pl.multiple_of(c * ck, ck)
        sel = (sc_ref[pl.ds(s0, ck), :] >= thr) & live
        lhs[:, 0:LANES] = jnp.where(sel, 0.0, NEG).astype(MXU_DTYPE)
        lhs[:, LANES:2 * LANES] = kb_ref[0, pl.ds(s0, ck), :]
        store_heads(buf, _dot_nt(lhs[...], rhs_ref[...]))

    def logits(buf, c, h, near):
        blk = buf[h]
        if near:
            bias = [nb_ref[jnp.clip(c * (ck // LANES) + j - (n_blk - 3), 0, 2), h] for j in range(ck // LANES)]
            blk = blk + jnp.concatenate(bias, axis=0)
        return blk

    def attend_chunk(c, c_next, buf, buf_next, lhs_next, near):
        if c_next is not None:
            qk_matmul(c_next, True, lhs_next, buf_next)
        for g in range(N_KV_HEADS):
            m_old = m_ref[g]
            m_new = jnp.concatenate(
                [jnp.maximum(m_old[:, r * tq:(r + 1) * tq],
                             jnp.max(logits(buf, c, g * GROUP + r, near), axis=0, keepdims=True))
                 for r in range(GROUP)], axis=1)
            scale = jnp.exp2(m_old - m_new)
            m_ref[g] = m_new
            probs = jnp.concatenate(
                [jnp.exp2(logits(buf, c, g * GROUP + r, near) - m_new[:, r * tq:(r + 1) * tq]).astype(MXU_DTYPE)
                 for r in range(GROUP)], axis=1)
            pv = _dot(vt_ref[0, g, c], probs)
            acc_ref[g] = scale * acc_ref[g] + pv

    qk_matmul(0, True, lhsa_ref, mma_ref)

    def attend_pair(p, near):
        c0 = 2 * p
        attend_chunk(c0, c0 + 1, mma_ref, mmb_ref, lhsb_ref, near)
        attend_chunk(c0 + 1, jnp.minimum(c0 + 2, last), mmb_ref, mma_ref, lhsa_ref, near)

    n_far = jnp.maximum(n_pair - 1, 0)

    def far_body(p, carry):
        attend_pair(p, near=False)
        return carry

    def near_body(p, carry):
        attend_pair(p, near=True)
        return carry

    lax.fori_loop(0, n_far, far_body, 0)
    lax.fori_loop(n_far, n_pair, near_body, 0)

    @pl.when(has_tail)
    def _():
        attend_chunk(last, None, mma_ref, None, None, near=True)

    for g in range(N_KV_HEADS):
        acc = acc_ref[g]
        o = acc[0:HEAD_DIM] / acc[HEAD_DIM:HEAD_DIM + 1]
        for r in range(GROUP):
            h = g * GROUP + r
            out_ref[0, :, h * HEAD_DIM:(h + 1) * HEAD_DIM] = o[:, r * tq:(r + 1) * tq].T.astype(out_ref.dtype)


def _attention(qi, q, wt, ki, kb, vt, nb, *, pos0, n_keys, k_top):
    bsz, _, seq, _ = q.shape
    tq, ck = LANES, KEY_CHUNK
    l_pad = ki.shape[1]
    n_q = seq // tq
    assert pos0 % LANES == 0 and seq % tq == 0 and l_pad % ck == 0
    assert pos0 + seq <= l_pad and n_keys <= l_pad and k_top <= ck
    qspec = lambda n, d: pl.BlockSpec((1, n, tq, d), lambda b, i: (b, 0, i, 0))
    return pl.pallas_call(
        functools.partial(_attn_kernel, pos0=pos0, n_keys=n_keys, k_top=k_top),
        out_shape=jax.ShapeDtypeStruct((bsz, seq, N_HEADS * HEAD_DIM), MXU_DTYPE),
        grid=(bsz, n_q),
        in_specs=[
            qspec(N_IDX_HEADS, IDX_DIM), qspec(N_HEADS, N_KV_HEADS * HEAD_DIM),
            pl.BlockSpec((1, N_IDX_HEADS, tq), lambda b, i: (b, 0, i)),
            pl.BlockSpec((1, l_pad, IDX_DIM), lambda b, i: (b, 0, 0)),
            pl.BlockSpec((1, l_pad, N_KV_HEADS * HEAD_DIM), lambda b, i: (b, 0, 0)),
            pl.BlockSpec((1, N_KV_HEADS, l_pad // ck, V_ROWS, ck), lambda b, i: (b, 0, 0, 0, 0)),
            pl.BlockSpec((3, N_HEADS, LANES, LANES), lambda b, i: (0, 0, 0, 0)),
        ],
        out_specs=pl.BlockSpec((1, tq, N_HEADS * HEAD_DIM), lambda b, i: (b, i, 0)),
        scratch_shapes=[
            pltpu.VMEM((l_pad, tq), jnp.float32),
            pltpu.VMEM((l_pad, tq), jnp.int16),
            pltpu.VMEM((l_pad, tq), jnp.int16),
            pltpu.VMEM((N_HEADS, ck, tq), jnp.float32),
            pltpu.VMEM((N_HEADS, ck, tq), jnp.float32),
            pltpu.VMEM((ck, 2 * LANES), MXU_DTYPE),
            pltpu.VMEM((ck, 2 * LANES), MXU_DTYPE),
            pltpu.VMEM((N_HEADS * tq, 2 * LANES), MXU_DTYPE),
            pltpu.VMEM((N_KV_HEADS, 1, GROUP * tq), jnp.float32),
            pltpu.VMEM((N_KV_HEADS, V_ROWS, GROUP * tq), jnp.float32),
        ],
        compiler_params=pltpu.CompilerParams(
            dimension_semantics=("arbitrary", "arbitrary"), vmem_limit_bytes=VMEM_LIMIT),
        name="dsa_attention",
    )(qi, q, wt, ki, kb, vt, nb)


def _layernorm(x, g, b):
    mu = jnp.mean(x, axis=-1, keepdims=True)
    xc = x - mu
    var = jnp.mean(xc * xc, axis=-1, keepdims=True)
    return xc * lax.rsqrt(var + LN_EPS) * g + b


def _mlp_kernel(x_ref, co_ref, at_ref, woa_ref, wob_ref, g1_ref, b1_ref, w1_ref, w2_ref, g2_ref, b2_ref,
                out_ref, *, ff_chunk):
    mix = _dot(co_ref[...], woa_ref[...]) + _dot(at_ref[...], wob_ref[...])
    x1 = _layernorm(ALPHA * x_ref[...] + mix, g1_ref[...], b1_ref[...])
    x1b = x1.astype(MXU_DTYPE)
    ff = jnp.zeros(x1.shape, jnp.float32)
    for c in range(D_FF // ff_chunk):
        hid = jnp.maximum(_dot(x1b, w1_ref[:, c * ff_chunk:(c + 1) * ff_chunk]), 0.0)
        ff = ff + _dot((hid * hid).astype(MXU_DTYPE), w2_ref[c * ff_chunk:(c + 1) * ff_chunk, :])
    out_ref[...] = _layernorm(ALPHA * x1 + ff, g2_ref[...], b2_ref[...])


def _mlp(x, conv_out, attn, woa, wob, g1, b1, w1, w2, g2, b2, *, tm, ff_chunk=1024):
    n_tok = x.shape[0]
    tok = lambda d: pl.BlockSpec((tm, d), lambda i: (i, 0))
    const = lambda shape: pl.BlockSpec(shape, lambda i: (0, 0), pipeline_mode=pl.Buffered(1))
    return pl.pallas_call(
        functools.partial(_mlp_kernel, ff_chunk=ff_chunk),
        out_shape=jax.ShapeDtypeStruct((n_tok, D_MODEL), jnp.float32),
        grid=(n_tok // tm,),
        in_specs=[
            tok(D_MODEL), tok(CONV_DIM), tok(N_HEADS * HEAD_DIM),
            const((CONV_DIM, D_MODEL)), const((N_HEADS * HEAD_DIM, D_MODEL)),
            const((1, D_MODEL)), const((1, D_MODEL)),
            const((D_MODEL, D_FF)), const((D_FF, D_MODEL)),
            const((1, D_MODEL)), const((1, D_MODEL)),
        ],
        out_specs=tok(D_MODEL),
        compiler_params=pltpu.CompilerParams(
            dimension_semantics=("arbitrary",), vmem_limit_bytes=VMEM_LIMIT),
        name="oproj_mlp",
    )(x, conv_out, attn, woa, wob, g1, b1, w1, w2, g2, b2)


def _tiles(seq):
    seq_pad = -(-seq // LANES) * LANES
    tm_in = KEY_CHUNK if seq_pad % KEY_CHUNK == 0 else LANES
    return seq_pad, tm_in


def _mlp_tile(n_tok):
    return 512 if n_tok % 512 == 0 else n_tok


def _layer(x, conv_prev, past, pos0, lw, nb):
    bsz, seq, _ = x.shape
    wa, wb, wc, conv_w, woa, wob, g1, b1, w1, w2, g2, b2 = lw
    seq_pad, tm_in = _tiles(seq)
    xp = x if seq_pad == seq else jnp.pad(x, ((0, 0), (0, seq_pad - seq), (0, 0)))
    (conv_out, q, kb, vt, k, v, qi, ki, kib, wt, new_conv) = _inproj(
        xp, conv_prev, wa, wb, wc, conv_w, tm=tm_in, n_valid=seq)
    k, v, ki = k[:, :seq], v[:, :seq], ki[:, :seq]
    ck = KEY_CHUNK
    if past is None:
        n_keys = seq
        assert seq_pad == seq and tm_in == ck
    else:
        pk, pv, pki = past
        n_keys = pk.shape[1] + seq
        l_pad = -(-(pos0 + seq_pad) // ck) * ck
        pad = l_pad - n_keys
        kb = jnp.pad(jnp.concatenate([pk.reshape(bsz, -1, N_KV_HEADS * HEAD_DIM), k], axis=1),
                     ((0, 0), (0, pad), (0, 0))).astype(MXU_DTYPE)
        v_all = jnp.pad(jnp.concatenate([pv.reshape(bsz, -1, N_KV_HEADS * HEAD_DIM), v], axis=1),
                        ((0, 0), (0, pad), (0, 0))).astype(MXU_DTYPE)
        kib = jnp.pad(jnp.concatenate([pki, ki], axis=1), ((0, 0), (0, pad), (0, 0))).astype(MXU_DTYPE)
        v_t =jnp.transpose(v_all.reshape(bsz, l_pad // ck, ck, N_KV_HEADS, HEAD_DIM), (0, 3, 1, 4, 2))
        ones = jnp.zeros((bsz, N_KV_HEADS, l_pad // ck, V_ROWS - HEAD_DIM, ck), MXU_DTYPE).at[:, :, :, 0].set(1)
        vt = jnp.concatenate([v_t, ones], axis=3)
    k_top = min(TOPK_MAX, n_keys // 4)
    attn = _attention(qi, q, wt, kib, kb, vt, nb, pos0=pos0, n_keys=n_keys, k_top=k_top)
    n_tok = bsz * seq
    y = _mlp(x.reshape(n_tok, D_MODEL), conv_out[:, :seq].reshape(n_tok, CONV_DIM),
             attn[:, :seq].reshape(n_tok, N_HEADS * HEAD_DIM), woa, wob, g1, b1, w1, w2, g2, b2,
             tm=_mlp_tile(n_tok))
    return (y.reshape(bsz, seq, D_MODEL),
            k.reshape(bsz, seq, N_KV_HEADS, HEAD_DIM), v.reshape(bsz, seq, N_KV_HEADS, HEAD_DIM), ki, new_conv)


def _layer_weights(l, w_in, conv_w, w_o, ln1_g, ln1_b, w_ff1, w_ff2, ln2_g, ln2_b):
    wi = w_in[l].astype(MXU_DTYPE)
    wa = wi[:, :3 * CONV_DIM]
    wb = wi[:, 3 * CONV_DIM:3 * CONV_DIM + D_QKV]
    wc = jnp.pad(wi[:, 3 * CONV_DIM + D_QKV:], ((0, 0), (0, D_IDX_PAD - D_IDX)))
    wo = w_o[l].astype(MXU_DTYPE)
    row = lambda a: a[l].reshape(1, D_MODEL)
    return (wa, wb, wc, conv_w[l], wo[:CONV_DIM], wo[CONV_DIM:], row(ln1_g), row(ln1_b),
            w_ff1[l].astype(MXU_DTYPE), w_ff2[l].astype(MXU_DTYPE), row(ln2_g), row(ln2_b))


def kernel(x_prompt, x_sample, cache_k, cache_v, cache_kidx, state_conv, w_in, conv_w, w_o, ln1_g, ln1_b,
           w_ff1, w_ff2, ln2_g, ln2_b, rel_bias):
    bp = x_prompt.shape[0]
    past_len = cache_k.shape[2]
    nb = _bias_tiles(rel_bias)
    hp, hs = x_prompt, x_sample
    outs = [[] for _ in range(8)]
    for l in range(DEPTH):
        lw = _layer_weights(l, w_in, conv_w, w_o, ln1_g, ln1_b, w_ff1, w_ff2, ln2_g, ln2_b)
        zero_conv = jnp.zeros((bp, CONV_WIDTH - 1, CONV_DIM), jnp.float32)
        hp, k_n, v_n, ki_n, c_n = _layer(hp, zero_conv, None, 0, lw, nb)
        for o, a in zip(outs[:4], (k_n, v_n, ki_n, c_n)):
            o.append(a)
        hs, k_n, v_n, ki_n, c_n = _layer(hs, state_conv[l], (cache_k[l], cache_v[l], cache_kidx[l]),
                                         past_len, lw, nb)
        for o, a in zip(outs[4:], (k_n, v_n, ki_n, c_n)):
            o.append(a)
    return (hp, hs) + tuple(jnp.stack(o) for o in outs)
```

```python
import functools
import math

import jax
import jax.numpy as jnp
from jax import lax
from jax.experimental import pallas as pl
from jax.experimental.pallas import tpu as pltpu

D_MODEL = 1024
CHUNK = 64
CONV_DIM = 512
CONV_WIDTH = 3
N_HEADS = 8
HEAD_DIM = 64
N_KV_HEADS = 2
GROUP = N_HEADS // N_KV_HEADS
N_IDX_HEADS = 8
IDX_DIM = 64
TOPK_MAX = 256
N_BUCKETS = 32
REL_MAX_DIST = 128
D_FF = 4 * D_MODEL
DEPTH = 2
ALPHA = (2 * DEPTH) ** 0.25
LN_EPS = 1e-5
NEG = -1e30
LOG2_E = math.log2(math.e)

LANES = 128
KEY_CHUNK = 512
V_ROWS = HEAD_DIM + 16
SENTINEL = float(jnp.finfo(jnp.float32).min)
M_INIT = -5e29
N_COUNTERS = 4

MXU_DTYPE = jnp.bfloat16
VMEM_LIMIT = 56 * 1024 * 1024

D_QKV = N_HEADS * HEAD_DIM + 2 * N_KV_HEADS * HEAD_DIM
D_IDX = N_IDX_HEADS * IDX_DIM + IDX_DIM + N_IDX_HEADS
D_IDX_PAD = 640


def _dot(a, b):
    return jnp.dot(a, b, preferred_element_type=jnp.float32)


def _dot_nt(a, b):
    return lax.dot_general(a, b, (((1,), (1,)), ((), ())), preferred_element_type=jnp.float32)


def _t5_bucket(rel):
    nb = N_BUCKETS // 2
    ret = (rel > 0).astype(jnp.int32) * nb
    n = jnp.abs(rel)
    max_exact = nb // 2
    nf = jnp.maximum(n, 1).astype(jnp.float32)
    large = max_exact + (jnp.log(nf / max_exact) / math.log(REL_MAX_DIST / max_exact)
                         * (nb - max_exact)).astype(jnp.int32)
    large = jnp.minimum(large, nb - 1)
    return ret + jnp.where(n < max_exact, n, large)


def _bias_kernel(rb_ref, out_ref):
    j = lax.broadcasted_iota(jnp.int32, (LANES, LANES), 0)
    i = lax.broadcasted_iota(jnp.int32, (LANES, LANES), 1)
    far_bucket = _t5_bucket(jnp.full((LANES, LANES), -(REL_MAX_DIST + 1), jnp.int32))
    buckets = [far_bucket, _t5_bucket(j - LANES - i), _t5_bucket(j - i)]
    for h in range(N_HEADS):
        vals = []
        for bk in buckets:
            v = jnp.zeros((LANES, LANES), jnp.float32)
            for b in range(N_BUCKETS):
                v = jnp.where(bk == b, rb_ref[b, h], v)
            vals.append(v)
        out_ref[0, h] = jnp.zeros((LANES, LANES), jnp.float32)
        out_ref[1, h] = (vals[1] - vals[0]) * LOG2_E
        out_ref[2, h] = (vals[2] - vals[0]) * LOG2_E


def _bias_tiles(rel_bias):
    return pl.pallas_call(
        _bias_kernel,
        out_shape=jax.ShapeDtypeStruct((3, N_HEADS, LANES, LANES), jnp.float32),
        in_specs=[pl.BlockSpec(memory_space=pltpu.SMEM)],
        name="rel_bias_tiles",
    )(rel_bias)


def _inproj_kernel(x_ref, cprev_ref, wa_ref, wb_ref, wc_ref, cw_ref,
                   co_ref, q_ref, kb_ref, vt_ref, k_ref, v_ref, qi_ref, ki_ref, kib_ref, wt_ref, nc_ref,
                   ubuf, *, tm, n_t, last_row):
    t = pl.program_id(1)
    xb = x_ref[0].astype(MXU_DTYPE)

    pa = _dot(xb, wa_ref[...])
    gb = pa[:, :CONV_DIM]
    u = pa[:, CONV_DIM:2 * CONV_DIM] * pa[:, 2 * CONV_DIM:]

    @pl.when(t == 0)
    def _():
        ubuf[0:8, :] = jnp.zeros((8, CONV_DIM), jnp.float32)
        ubuf[6:8, :] = cprev_ref[0]

    ubuf[8:8 + tm, :] = u
    cw = cw_ref[...]
    y = cw[0:1, :] * ubuf[6:6 + tm, :] + cw[1:2, :] * ubuf[7:7 + tm, :]
    y = y + cw[2:3, :] * u
    co_ref[0] = (gb * y).astype(co_ref.dtype)

    @pl.when(t == n_t - 1)
    def _():
        nc_ref[0] = ubuf[8 + last_row - 1:8 + last_row + 1, :]

    ubuf[0:8, :] = ubuf[tm:tm + 8, :]

    pb = _dot(xb, wb_ref[...])
    q = pb[:, :N_HEADS * HEAD_DIM] * (HEAD_DIM ** -0.5 * LOG2_E)
    zero_half = jnp.zeros((tm, HEAD_DIM), q_ref.dtype)
    for h in range(N_HEADS):
        qh = q[:, h * HEAD_DIM:(h + 1) * HEAD_DIM].astype(q_ref.dtype)
        halves = [zero_half] * N_KV_HEADS
        halves[h // GROUP] = qh
        q_ref[0, h] = jnp.concatenate(halves, axis=1)
    k = pb[:, N_HEADS * HEAD_DIM:N_HEADS * HEAD_DIM + N_KV_HEADS * HEAD_DIM]
    v = pb[:, N_HEADS * HEAD_DIM + N_KV_HEADS * HEAD_DIM:]
    k_ref[0] = k
    v_ref[0] = v
    kb_ref[0] = k.astype(kb_ref.dtype)
    vt = v.T
    ones_rows = (lax.broadcasted_iota(jnp.int32, (V_ROWS - HEAD_DIM, tm), 0) == 0).astype(vt_ref.dtype)
    for g in range(N_KV_HEADS):
        vt_ref[0, g, 0, 0:HEAD_DIM, :] = vt[g * HEAD_DIM:(g + 1) * HEAD_DIM].astype(vt_ref.dtype)
        vt_ref[0, g, 0, HEAD_DIM:V_ROWS, :] = ones_rows

    pc = _dot(xb, wc_ref[...])
    for h in range(N_IDX_HEADS):
        qi_ref[0, h] = pc[:, h * IDX_DIM:(h + 1) * IDX_DIM].astype(qi_ref.dtype)
    ki = pc[:, N_IDX_HEADS * IDX_DIM:N_IDX_HEADS * IDX_DIM + IDX_DIM]
    ki_ref[0] = ki
    kib_ref[0] = ki.astype(kib_ref.dtype)
    tail_t = pc[:, N_IDX_HEADS * IDX_DIM:].T
    wt_ref[0] = tail_t[IDX_DIM:IDX_DIM + N_IDX_HEADS] * (N_IDX_HEADS ** -0.5) * (IDX_DIM ** -0.5)


def _inproj(x, conv_prev, wa, wb, wc, conv_w, *, tm, n_valid):
    bsz, seq, _ = x.shape
    n_t = seq // tm
    f32 = jnp.float32
    const = lambda b, t: (0, 0)
    out_shape = (
        jax.ShapeDtypeStruct((bsz, seq, CONV_DIM), MXU_DTYPE),
        jax.ShapeDtypeStruct((bsz, N_HEADS, seq, N_KV_HEADS * HEAD_DIM), MXU_DTYPE),
        jax.ShapeDtypeStruct((bsz, seq, N_KV_HEADS * HEAD_DIM), MXU_DTYPE),
        jax.ShapeDtypeStruct((bsz, N_KV_HEADS, n_t, V_ROWS, tm), MXU_DTYPE),
        jax.ShapeDtypeStruct((bsz, seq, N_KV_HEADS * HEAD_DIM), f32),
        jax.ShapeDtypeStruct((bsz, seq, N_KV_HEADS * HEAD_DIM), f32),
        jax.ShapeDtypeStruct((bsz, N_IDX_HEADS, seq, IDX_DIM), MXU_DTYPE),
        jax.ShapeDtypeStruct((bsz, seq, IDX_DIM), f32),
        jax.ShapeDtypeStruct((bsz, seq, IDX_DIM), MXU_DTYPE),
        jax.ShapeDtypeStruct((bsz, N_IDX_HEADS, seq), f32),
        jax.ShapeDtypeStruct((bsz, CONV_WIDTH - 1, CONV_DIM), f32),
    )
    tok = lambda d: pl.BlockSpec((1, tm, d), lambda b, t: (b, t, 0))
    hm = lambda n, d: pl.BlockSpec((1, n, tm, d), lambda b, t: (b, 0, t, 0))
    state = pl.BlockSpec((1, CONV_WIDTH - 1, CONV_DIM), lambda b, t: (b, 0, 0))
    return pl.pallas_call(
        functools.partial(_inproj_kernel, tm=tm, n_t=n_t, last_row=n_valid - 1 - (n_t - 1) * tm),
        out_shape=out_shape,
        grid=(bsz, n_t),
        in_specs=[
            tok(D_MODEL), state,
            pl.BlockSpec((D_MODEL, 3 * CONV_DIM), const),
            pl.BlockSpec((D_MODEL, D_QKV), const),
            pl.BlockSpec((D_MODEL, D_IDX_PAD), const),
            pl.BlockSpec((CONV_WIDTH, CONV_DIM), const),
        ],
        out_specs=(
            tok(CONV_DIM), hm(N_HEADS, N_KV_HEADS * HEAD_DIM), tok(N_KV_HEADS * HEAD_DIM),
            pl.BlockSpec((1, N_KV_HEADS, 1, V_ROWS, tm), lambda b, t: (b, 0, t, 0, 0)),
            tok(N_KV_HEADS * HEAD_DIM), tok(N_KV_HEADS * HEAD_DIM), hm(N_IDX_HEADS, IDX_DIM),
            tok(IDX_DIM), tok(IDX_DIM),
            pl.BlockSpec((1, N_IDX_HEADS, tm), lambda b, t: (b, 0, t)), state,
        ),
        scratch_shapes=[pltpu.VMEM((tm + 8, CONV_DIM), f32)],
        compiler_params=pltpu.CompilerParams(
            dimension_semantics=("arbitrary", "arbitrary"), vmem_limit_bytes=VMEM_LIMIT),
        name="inproj_conv",
    )(x, conv_prev, wa, wb, wc, conv_w)


def _ordered_key(x):
    bits = lax.bitcast_convert_type(x, jnp.int32)
    return bits ^ (lax.shift_right_arithmetic(bits, 31) & jnp.int32(0x7FFFFFFF))


def _attn_kernel(qi_ref, q_ref, wt_ref, ki_ref, kb_ref, vt_ref, nb_ref, out_ref,
                 sc_ref, hi_ref, lo_ref, mma_ref, mmb_ref, lhsa_ref, lhsb_ref, rhs_ref, rhst_ref, m_ref, acc_ref, *,
                 pos0, n_keys, k_top):
    tq, ck = LANES, KEY_CHUNK
    p0 = pos0 + pl.program_id(1) * tq
    n_blk = p0 // LANES + 1
    n_chunk = (n_blk * LANES + ck - 1) // ck
    q_pos = p0 + lax.broadcasted_iota(jnp.int32, (1, tq), 1)
    limit = jnp.minimum((q_pos // CHUNK + 1) * CHUNK, n_keys)
    row = lax.broadcasted_iota(jnp.int32, (ck, tq), 0)

    wt = wt_ref[0]
    qi_all = qi_ref[0].reshape(N_IDX_HEADS * tq, IDX_DIM)

    last = n_chunk - 1
    n_pair = n_chunk // 2
    has_tail = n_chunk % 2 == 1

    def store_heads(buf, res):
        for h in range(N_HEADS):
            buf[h] = res[:, h * tq:(h + 1) * tq]

    def idx_matmul(c, buf):
        s0 = pl.multiple_of(c * ck, ck)
        store_heads(buf, _dot_nt(ki_ref[0, pl.ds(s0, ck), :], qi_all))

    def idx_scores(c, buf):
        s0 = pl.multiple_of(c * ck, ck)
        acc = jnp.zeros((ck, tq), jnp.float32)
        for h in range(N_IDX_HEADS):
            acc = acc + jnp.maximum(buf[h], 0.0) * wt[h:h + 1, :]
        score = jnp.where(s0 + row < limit, acc, SENTINEL)
        sc_ref[pl.ds(s0, ck), :] = score
        key = _ordered_key(score)
        hi_ref[pl.ds(s0, ck), :] = lax.shift_right_arithmetic(key, 16).astype(jnp.int16)
        lo_ref[pl.ds(s0, ck), :] = ((key & jnp.int32(0xFFFF)) - 32768).astype(jnp.int16)

    idx_matmul(0, mma_ref)

    def score_pair(p, carry):
        c0 = 2 * p
        idx_matmul(c0 + 1, mmb_ref)
        idx_scores(c0, mma_ref)
        idx_matmul(jnp.minimum(c0 + 2, last), mma_ref)
        idx_scores(c0 + 1, mmb_ref)
        return carry

    lax.fori_loop(0, n_pair, score_pair, 0)

    @pl.when(has_tail)
    def _():
        idx_scores(last, mma_ref)

    def count_ge(ref, cand):
        thr = jnp.broadcast_to(cand, (16, tq)).astype(jnp.int16)

        def body(c, cnts):
            chunk = ref[pl.ds(pl.multiple_of(c * ck, ck), ck), :]
            cnts = list(cnts)
            for j in range(ck // 16):
                hit = jnp.where(chunk[j * 16:(j + 1) * 16] >= thr, jnp.int16(1), jnp.int16(0))
                cnts[j % N_COUNTERS] = cnts[j % N_COUNTERS] + hit
            return tuple(cnts)
        zero = jnp.zeros((16, tq), jnp.int16)
        cnts = lax.fori_loop(0, n_chunk, body, (zero,) * N_COUNTERS)
        cnt = sum(c.astype(jnp.int32) for c in cnts)
        return jnp.sum(cnt, axis=0, keepdims=True)

    def bisect(ref, cnt_all):
        def step(it, carry):
            off, cnt_at = carry
            cand = off | lax.shift_left(jnp.int32(1), 15 - it)
            cnt = count_ge(ref, cand - 32768)
            ok = cnt >= k_top
            return jnp.where(ok, cand, off), jnp.where(ok, cnt, cnt_at)
        off, cnt_at = lax.fori_loop(0, 16, step, (jnp.zeros((1, tq), jnp.int32), cnt_all))
        return off - 32768, cnt_at

    cnt_all = jnp.full((1, tq), 1, jnp.int32) * (n_chunk * ck)
    t_hi, cnt_hi = bisect(hi_ref, cnt_all)

    def pin_chunk(c, carry):
        sl = pl.ds(pl.multiple_of(c * ck, ck), ck)
        hi = hi_ref[sl, :]
        lo_ref[sl, :] = jnp.where(hi > t_hi[0:1].astype(jnp.int16), jnp.int16(32767),
                                  jnp.where(hi < t_hi[0:1].astype(jnp.int16), jnp.int16(-32768), lo_ref[sl, :]))
        return carry

    lax.fori_loop(0, n_chunk, pin_chunk, 0)
    t_lo, cnt_ge = bisect(lo_ref, cnt_hi)
    thr_key = lax.shift_left(t_hi, 16) | (t_lo + 32768)
    thr = lax.bitcast_convert_type(
        thr_key ^ (lax.shift_right_arithmetic(thr_key, 31) & jnp.int32(0x7FFFFFFF)), jnp.float32)
    sent_next = jnp.nextafter(jnp.float32(SENTINEL), jnp.float32(0.0))
    has_k = thr > SENTINEL
    thr = jnp.maximum(thr, sent_next)

    excess = jnp.where(has_k & (cnt_ge > k_top), 1, 0)

    @pl.when(jnp.sum(excess) > 0)
    def _():
        def count_gt(c, cnt):
            s0 = pl.multiple_of(c * ck, ck)
            return cnt + jnp.sum(jnp.where(sc_ref[pl.ds(s0, ck), :] > thr, 1, 0), axis=0, keepdims=True)
        need = k_top - lax.fori_loop(0, n_chunk, count_gt, jnp.zeros((1, tq), jnp.int32))

        def ties_before(bound):
            def body(c, cnt):
                s0 = pl.multiple_of(c * ck, ck)
                tie = (sc_ref[pl.ds(s0, ck), :] == thr) & (s0 + row < bound)
                return cnt + jnp.sum(jnp.where(tie, 1, 0), axis=0, keepdims=True)
            return lax.fori_loop(0, n_chunk, body, jnp.zeros((1, tq), jnp.int32))

        n_bits = max(1, (n_keys + ck).bit_length())

        def cut_step(it, cut):
            cand = cut | lax.shift_left(jnp.int32(1), n_bits - 1 - it)
            return jnp.where(ties_before(cand) <= need, cand, cut)

        cut = lax.fori_loop(0, n_bits, cut_step, jnp.zeros((1, tq), jnp.int32))

        def drop(c, carry):
            s0 = pl.multiple_of(c * ck, ck)
            sc = sc_ref[pl.ds(s0, ck), :]
            sc_ref[pl.ds(s0, ck), :] = jnp.where((sc == thr) & (s0 + row >= cut), SENTINEL, sc)
            return carry
        lax.fori_loop(0, n_chunk, drop, 0)

    m_ref[...] = jnp.full(m_ref.shape, M_INIT, jnp.float32)
    acc_ref[...] = jnp.zeros(acc_ref.shape, jnp.float32)

    eye = (lax.broadcasted_iota(jnp.int32, (tq, tq), 0) == lax.broadcasted_iota(jnp.int32, (tq, tq), 1))
    for h in range(N_HEADS):
        rhs_ref[h * tq:(h + 1) * tq, 0:LANES] = eye.astype(MXU_DTYPE)
        rhs_ref[h * tq:(h + 1) * tq, LANES:2 * LANES] = q_ref[0, h]
    rhst_ref[...] = rhs_ref[...].T

    def qk_matmul(c, live, lhs, buf):
        s0 = pl.multiple_of(c * ck, ck)
        sel = (sc_ref[pl.ds(s0, ck), :] >= thr) & live
        lhs[:, 0:LANES] = jnp.where(sel, 0.0, NEG).astype(MXU_DTYPE)
        lhs[:, LANES:2 * LANES] = kb_ref[0, pl.ds(s0, ck), :]
        store_heads(buf, _dot(lhs[...], rhst_ref[...]))

    def logits(buf, c, h, near):
        blk = buf[h]
        if near:
            bias = [nb_ref[jnp.clip(c * (ck // LANES) + j - (n_blk - 3), 0, 2), h] for j in range(ck // LANES)]
            blk = blk + jnp.concatenate(bias, axis=0)
        return blk

    def attend_chunk(c, c_next, buf, buf_next, lhs_next, near):
        if c_next is not None:
            qk_matmul(c_next, True, lhs_next, buf_next)
        for g in range(N_KV_HEADS):
            m_old = m_ref[g]
            m_new = jnp.concatenate(
                [jnp.maximum(m_old[:, r * tq:(r + 1) * tq],
                             jnp.max(logits(buf, c, g * GROUP + r, near), axis=0, keepdims=True))
                 for r in range(GROUP)], axis=1)
            scale = jnp.exp2(m_old - m_new)
            m_ref[g] = m_new
            probs = jnp.concatenate(
                [jnp.exp2(logits(buf, c, g * GROUP + r, near) - m_new[:, r * tq:(r + 1) * tq]).astype(MXU_DTYPE)
                 for r in range(GROUP)], axis=1)
            pv = _dot(vt_ref[0, g, c], probs)
            acc_ref[g] = scale * acc_ref[g] + pv

    qk_matmul(0, True, lhsa_ref, mma_ref)

    def attend_pair(p, near):
        c0 = 2 * p
        attend_chunk(c0, c0 + 1, mma_ref, mmb_ref, lhsb_ref, near)
        attend_chunk(c0 + 1, jnp.minimum(c0 + 2, last), mmb_ref, mma_ref, lhsa_ref, near)

    n_far = jnp.maximum(n_pair - 1, 0)

    def far_body(p, carry):
        attend_pair(p, near=False)
        return carry

    def near_body(p, carry):
        attend_pair(p, near=True)
        return carry

    lax.fori_loop(0, n_far, far_body, 0)
    lax.fori_loop(n_far, n_pair, near_body, 0)

    @pl.when(has_tail)
    def _():
        attend_chunk(last, None, mma_ref, None, None, near=True)

    for g in range(N_KV_HEADS):
        acc = acc_ref[g]
        o = acc[0:HEAD_DIM] / acc[HEAD_DIM:HEAD_DIM + 1]
        for r in range(GROUP):
            h = g * GROUP + r
            out_ref[0, :, h * HEAD_DIM:(h + 1) * HEAD_DIM] = o[:, r * tq:(r + 1) * tq].T.astype(out_ref.dtype)


def _attention(qi, q, wt, ki, kb, vt, nb, *, pos0, n_keys, k_top):
    bsz, _, seq, _ = q.shape
    tq, ck = LANES, KEY_CHUNK
    l_pad = ki.shape[1]
    n_q = seq // tq
    assert pos0 % LANES == 0 and seq % tq == 0 and l_pad % ck == 0
    assert pos0 + seq <= l_pad and n_keys <= l_pad and k_top <= ck
    qspec = lambda n, d: pl.BlockSpec((1, n, tq, d), lambda b, i: (b, 0, i, 0))
    return pl.pallas_call(
        functools.partial(_attn_kernel, pos0=pos0, n_keys=n_keys, k_top=k_top),
        out_shape=jax.ShapeDtypeStruct((bsz, seq, N_HEADS * HEAD_DIM), MXU_DTYPE),
        grid=(bsz, n_q),
        in_specs=[
            qspec(N_IDX_HEADS, IDX_DIM), qspec(N_HEADS, N_KV_HEADS * HEAD_DIM),
            pl.BlockSpec((1, N_IDX_HEADS, tq), lambda b, i: (b, 0, i)),
            pl.BlockSpec((1, l_pad, IDX_DIM), lambda b, i: (b, 0, 0)),
            pl.BlockSpec((1, l_pad, N_KV_HEADS * HEAD_DIM), lambda b, i: (b, 0, 0)),
            pl.BlockSpec((1, N_KV_HEADS, l_pad // ck, V_ROWS, ck), lambda b, i: (b, 0, 0, 0, 0)),
            pl.BlockSpec((3, N_HEADS, LANES, LANES), lambda b, i: (0, 0, 0, 0)),
        ],
        out_specs=pl.BlockSpec((1, tq, N_HEADS * HEAD_DIM), lambda b, i: (b, i, 0)),
        scratch_shapes=[
            pltpu.VMEM((l_pad, tq), jnp.float32),
            pltpu.VMEM((l_pad, tq), jnp.int16),
            pltpu.VMEM((l_pad, tq), jnp.int16),
            pltpu.VMEM((N_HEADS, ck, tq), jnp.float32),
            pltpu.VMEM((N_HEADS, ck, tq), jnp.float32),
            pltpu.VMEM((ck, 2 * LANES), MXU_DTYPE),
            pltpu.VMEM((ck, 2 * LANES), MXU_DTYPE),
            pltpu.VMEM((N_HEADS * tq, 2 * LANES), MXU_DTYPE),
            pltpu.VMEM((2 * LANES, N_HEADS * tq), MXU_DTYPE),
            pltpu.VMEM((N_KV_HEADS, 1, GROUP * tq), jnp.float32),
            pltpu.VMEM((N_KV_HEADS, V_ROWS, GROUP * tq), jnp.float32),
        ],
        compiler_params=pltpu.CompilerParams(
            dimension_semantics=("arbitrary", "arbitrary"), vmem_limit_bytes=VMEM_LIMIT),
        name="dsa_attention",
    )(qi, q, wt, ki, kb, vt, nb)


def _layernorm(x, g, b):
    mu = jnp.mean(x, axis=-1, keepdims=True)
    xc = x - mu
    var = jnp.mean(xc * xc, axis=-1, keepdims=True)
    return xc * lax.rsqrt(var + LN_EPS) * g + b


def _mlp_kernel(x_ref, co_ref, at_ref, woa_ref, wob_ref, g1_ref, b1_ref, w1_ref, w2_ref, g2_ref, b2_ref,
                out_ref, *, ff_chunk):
    mix = _dot(co_ref[...], woa_ref[...]) + _dot(at_ref[...], wob_ref[...])
    x1 = _layernorm(ALPHA * x_ref[...] + mix, g1_ref[...], b1_ref[...])
    x1b = x1.astype(MXU_DTYPE)
    ff = jnp.zeros(x1.shape, jnp.float32)
    for c in range(D_FF // ff_chunk):
        hid = jnp.maximum(_dot(x1b, w1_ref[:, c * ff_chunk:(c + 1) * ff_chunk]), 0.0)
        ff = ff + _dot((hid * hid).astype(MXU_DTYPE), w2_ref[c * ff_chunk:(c + 1) * ff_chunk, :])
    out_ref[...] = _layernorm(ALPHA * x1 + ff, g2_ref[...], b2_ref[...])


def _mlp(x, conv_out, attn, woa, wob, g1, b1, w1, w2, g2, b2, *, tm, ff_chunk=1024):
    n_tok = x.shape[0]
    tok = lambda d: pl.BlockSpec((tm, d), lambda i: (i, 0))
    const = lambda shape: pl.BlockSpec(shape, lambda i: (0, 0), pipeline_mode=pl.Buffered(1))
    return pl.pallas_call(
        functools.partial(_mlp_kernel, ff_chunk=ff_chunk),
        out_shape=jax.ShapeDtypeStruct((n_tok, D_MODEL), jnp.float32),
        grid=(n_tok // tm,),
        in_specs=[
            tok(D_MODEL), tok(CONV_DIM), tok(N_HEADS * HEAD_DIM),
            const((CONV_DIM, D_MODEL)), const((N_HEADS * HEAD_DIM, D_MODEL)),
            const((1, D_MODEL)), const((1, D_MODEL)),
            const((D_MODEL, D_FF)), const((D_FF, D_MODEL)),
            const((1, D_MODEL)), const((1, D_MODEL)),
        ],
        out_specs=tok(D_MODEL),
        compiler_params=pltpu.CompilerParams(
            dimension_semantics=("arbitrary",), vmem_limit_bytes=VMEM_LIMIT),
        name="oproj_mlp",
    )(x, conv_out, attn, woa, wob, g1, b1, w1, w2, g2, b2)


def _tiles(seq):
    seq_pad = -(-seq // LANES) * LANES
    tm_in = KEY_CHUNK if seq_pad % KEY_CHUNK == 0 else LANES
    return seq_pad, tm_in


def _mlp_tile(n_tok):
    return 512 if n_tok % 512 == 0 else n_tok


def _layer(x, conv_prev, past, pos0, lw, nb):
    bsz, seq, _ = x.shape
    wa, wb, wc, conv_w, woa, wob, g1, b1, w1, w2, g2, b2 = lw
    seq_pad, tm_in = _tiles(seq)
    xp = x if seq_pad == seq else jnp.pad(x, ((0, 0), (0, seq_pad - seq), (0, 0)))
    (conv_out, q, kb, vt, k, v, qi, ki, kib, wt, new_conv) = _inproj(
        xp, conv_prev, wa, wb, wc, conv_w, tm=tm_in, n_valid=seq)
    k, v, ki = k[:, :seq], v[:, :seq], ki[:, :seq]
    ck = KEY_CHUNK
    if past is None:
        n_keys = seq
        assert seq_pad == seq and tm_in == ck
    else:
        pk, pv, pki = past
        n_keys = pk.shape[1] + seq
        l_pad = -(-(pos0 + seq_pad) // ck) * ck
        pad = l_pad - n_keys
        kb = jnp.pad(jnp.concatenate([pk.reshape(bsz, -1, N_KV_HEADS * HEAD_DIM), k], axis=1),
                     ((0, 0), (0, pad), (0, 0))).astype(MXU_DTYPE)
        v_all = jnp.pad(jnp.concatenate([pv.reshape(bsz, -1, N_KV_HEADS * HEAD_DIM), v], axis=1),
                        ((0, 0), (0, pad), (0, 0))).astype(MXU_DTYPE)
        kib = jnp.pad(jnp.concatenate([pki, ki], axis=1), ((0, 0), (0, pad), (0, 0))).astype(MXU_DTYPE)
        v_t =jnp.transpose(v_all.reshape(bsz, l_pad // ck, ck, N_KV_HEADS, HEAD_DIM), (0, 3, 1, 4, 2))
        ones = jnp.zeros((bsz, N_KV_HEADS, l_pad // ck, V_ROWS - HEAD_DIM, ck), MXU_DTYPE).at[:, :, :, 0].set(1)
        vt = jnp.concatenate([v_t, ones], axis=3)
    k_top = min(TOPK_MAX, n_keys // 4)
    attn = _attention(qi, q, wt, kib, kb, vt, nb, pos0=pos0, n_keys=n_keys, k_top=k_top)
    n_tok = bsz * seq
    y = _mlp(x.reshape(n_tok, D_MODEL), conv_out[:, :seq].reshape(n_tok, CONV_DIM),
             attn[:, :seq].reshape(n_tok, N_HEADS * HEAD_DIM), woa, wob, g1, b1, w1, w2, g2, b2,
             tm=_mlp_tile(n_tok))
    return (y.reshape(bsz, seq, D_MODEL),
            k.reshape(bsz, seq, N_KV_HEADS, HEAD_DIM), v.reshape(bsz, seq, N_KV_HEADS, HEAD_DIM), ki, new_conv)


def _layer_weights(l, w_in, conv_w, w_o, ln1_g, ln1_b, w_ff1, w_ff2, ln2_g, ln2_b):
    wi = w_in[l].astype(MXU_DTYPE)
    wa = wi[:, :3 * CONV_DIM]
    wb = wi[:, 3 * CONV_DIM:3 * CONV_DIM + D_QKV]
    wc = jnp.pad(wi[:, 3 * CONV_DIM + D_QKV:], ((0, 0), (0, D_IDX_PAD - D_IDX)))
    wo = w_o[l].astype(MXU_DTYPE)
    row = lambda a: a[l].reshape(1, D_MODEL)
    return (wa, wb, wc, conv_w[l], wo[:CONV_DIM], wo[CONV_DIM:], row(ln1_g), row(ln1_b),
            w_ff1[l].astype(MXU_DTYPE), w_ff2[l].astype(MXU_DTYPE), row(ln2_g), row(ln2_b))


def kernel(x_prompt, x_sample, cache_k, cache_v, cache_kidx, state_conv, w_in, conv_w, w_o, ln1_g, ln1_b,
           w_ff1, w_ff2, ln2_g, ln2_b, rel_bias):
    bp = x_prompt.shape[0]
    past_len = cache_k.shape[2]
    nb = _bias_tiles(rel_bias)
    hp, hs = x_prompt, x_sample
    outs = [[] for _ in range(8)]
    for l in range(DEPTH):
        lw = _layer_weights(l, w_in, conv_w, w_o, ln1_g, ln1_b, w_ff1, w_ff2, ln2_g, ln2_b)
        zero_conv = jnp.zeros((bp, CONV_WIDTH - 1, CONV_DIM), jnp.float32)
        hp, k_n, v_n, ki_n, c_n = _layer(hp, zero_conv, None, 0, lw, nb)
        for o, a in zip(outs[:4], (k_n, v_n, ki_n, c_n)):
            o.append(a)
        hs, k_n, v_n, ki_n, c_n = _layer(hs, state_conv[l], (cache_k[l], cache_v[l], cache_kidx[l]),
                                         past_len, lw, nb)
        for o, a in zip(outs[4:], (k_n, v_n, ki_n, c_n)):
            o.append(a)
    return (hp, hs) + tuple(jnp.stack(o) for o in outs)
```

```python
import functools
import math

import jax
import jax.numpy as jnp
from jax import lax
from jax.experimental import pallas as pl
from jax.experimental.pallas import tpu as pltpu

D_MODEL = 1024
CHUNK = 64
CONV_DIM = 512
CONV_WIDTH = 3
N_HEADS = 8
HEAD_DIM = 64
N_KV_HEADS = 2
GROUP = N_HEADS // N_KV_HEADS
N_IDX_HEADS = 8
IDX_DIM = 64
TOPK_MAX = 256
N_BUCKETS = 32
REL_MAX_DIST = 128
D_FF = 4 * D_MODEL
DEPTH = 2
ALPHA = (2 * DEPTH) ** 0.25
LN_EPS = 1e-5
NEG = -1e30
LOG2_E = math.log2(math.e)

LANES = 128
KEY_CHUNK = 512
V_ROWS = HEAD_DIM + 16
SENTINEL = float(jnp.finfo(jnp.float32).min)
M_INIT = -5e29
N_COUNTERS = 4

MXU_DTYPE = jnp.bfloat16
VMEM_LIMIT = 56 * 1024 * 1024

D_QKV = N_HEADS * HEAD_DIM + 2 * N_KV_HEADS * HEAD_DIM
D_IDX = N_IDX_HEADS * IDX_DIM + IDX_DIM + N_IDX_HEADS
D_IDX_PAD = 640


def _dot(a, b):
    return jnp.dot(a, b, preferred_element_type=jnp.float32)


def _dot_nt(a, b):
    return lax.dot_general(a, b, (((1,), (1,)), ((), ())), preferred_element_type=jnp.float32)


def _t5_bucket(rel):
    nb = N_BUCKETS // 2
    ret = (rel > 0).astype(jnp.int32) * nb
    n = jnp.abs(rel)
    max_exact = nb // 2
    nf = jnp.maximum(n, 1).astype(jnp.float32)
    large = max_exact + (jnp.log(nf / max_exact) / math.log(REL_MAX_DIST / max_exact)
                         * (nb - max_exact)).astype(jnp.int32)
    large = jnp.minimum(large, nb - 1)
    return ret + jnp.where(n < max_exact, n, large)


def _bias_kernel(rb_ref, out_ref):
    j = lax.broadcasted_iota(jnp.int32, (LANES, LANES), 0)
    i = lax.broadcasted_iota(jnp.int32, (LANES, LANES), 1)
    far_bucket = _t5_bucket(jnp.full((LANES, LANES), -(REL_MAX_DIST + 1), jnp.int32))
    buckets = [far_bucket, _t5_bucket(j - LANES - i), _t5_bucket(j - i)]
    for h in range(N_HEADS):
        vals = []
        for bk in buckets:
            v = jnp.zeros((LANES, LANES), jnp.float32)
            for b in range(N_BUCKETS):
                v = jnp.where(bk == b, rb_ref[b, h], v)
            vals.append(v)
        out_ref[0, h] = jnp.zeros((LANES, LANES), jnp.float32)
        out_ref[1, h] = (vals[1] - vals[0]) * LOG2_E
        out_ref[2, h] = (vals[2] - vals[0]) * LOG2_E


def _bias_tiles(rel_bias):
    return pl.pallas_call(
        _bias_kernel,
        out_shape=jax.ShapeDtypeStruct((3, N_HEADS, LANES, LANES), jnp.float32),
        in_specs=[pl.BlockSpec(memory_space=pltpu.SMEM)],
        name="rel_bias_tiles",
    )(rel_bias)


def _inproj_kernel(x_ref, cprev_ref, wa_ref, wb_ref, wc_ref, cw_ref,
                   co_ref, q_ref, kb_ref, vt_ref, k_ref, v_ref, qi_ref, ki_ref, kib_ref, wt_ref, nc_ref,
                   ubuf, *, tm, n_t, last_row):
    t = pl.program_id(1)
    xb = x_ref[0].astype(MXU_DTYPE)

    pa = _dot(xb, wa_ref[...])
    gb = pa[:, :CONV_DIM]
    u = pa[:, CONV_DIM:2 * CONV_DIM] * pa[:, 2 * CONV_DIM:]

    @pl.when(t == 0)
    def _():
        ubuf[0:8, :] = jnp.zeros((8, CONV_DIM), jnp.float32)
        ubuf[6:8, :] = cprev_ref[0]

    ubuf[8:8 + tm, :] = u
    cw = cw_ref[...]
    y = cw[0:1, :] * ubuf[6:6 + tm, :] + cw[1:2, :] * ubuf[7:7 + tm, :]
    y = y + cw[2:3, :] * u
    co_ref[0] = (gb * y).astype(co_ref.dtype)

    @pl.when(t == n_t - 1)
    def _():
        nc_ref[0] = ubuf[8 + last_row - 1:8 + last_row + 1, :]

    ubuf[0:8, :] = ubuf[tm:tm + 8, :]

    pb = _dot(xb, wb_ref[...])
    q = pb[:, :N_HEADS * HEAD_DIM] * (HEAD_DIM ** -0.5 * LOG2_E)
    zero_half = jnp.zeros((tm, HEAD_DIM), q_ref.dtype)
    for h in range(N_HEADS):
        qh = q[:, h * HEAD_DIM:(h + 1) * HEAD_DIM].astype(q_ref.dtype)
        halves = [zero_half] * N_KV_HEADS
        halves[h // GROUP] = qh
        q_ref[0, h] = jnp.concatenate(halves, axis=1)
    k = pb[:, N_HEADS * HEAD_DIM:N_HEADS * HEAD_DIM + N_KV_HEADS * HEAD_DIM]
    v = pb[:, N_HEADS * HEAD_DIM + N_KV_HEADS * HEAD_DIM:]
    k_ref[0] = k
    v_ref[0] = v
    kb_ref[0] = k.astype(kb_ref.dtype)
    vt = v.T
    ones_rows = (lax.broadcasted_iota(jnp.int32, (V_ROWS - HEAD_DIM, tm), 0) == 0).astype(vt_ref.dtype)
    for g in range(N_KV_HEADS):
        vt_ref[0, g, 0, 0:HEAD_DIM, :] = vt[g * HEAD_DIM:(g + 1) * HEAD_DIM].astype(vt_ref.dtype)
        vt_ref[0, g, 0, HEAD_DIM:V_ROWS, :] = ones_rows

    pc = _dot(xb, wc_ref[...])
    for h in range(N_IDX_HEADS):
        qi_ref[0, h] = pc[:, h * IDX_DIM:(h + 1) * IDX_DIM].astype(qi_ref.dtype)
    ki = pc[:, N_IDX_HEADS * IDX_DIM:N_IDX_HEADS * IDX_DIM + IDX_DIM]
    ki_ref[0] = ki
    kib_ref[0] = ki.astype(kib_ref.dtype)
    tail_t = pc[:, N_IDX_HEADS * IDX_DIM:].T
    wt_ref[0] = tail_t[IDX_DIM:IDX_DIM + N_IDX_HEADS] * (N_IDX_HEADS ** -0.5) * (IDX_DIM ** -0.5)


def _inproj(x, conv_prev, wa, wb, wc, conv_w, *, tm, n_valid):
    bsz, seq, _ = x.shape
    n_t = seq // tm
    f32 = jnp.float32
    const = lambda b, t: (0, 0)
    out_shape = (
        jax.ShapeDtypeStruct((bsz, seq, CONV_DIM), MXU_DTYPE),
        jax.ShapeDtypeStruct((bsz, N_HEADS, seq, N_KV_HEADS * HEAD_DIM), MXU_DTYPE),
        jax.ShapeDtypeStruct((bsz, seq, N_KV_HEADS * HEAD_DIM), MXU_DTYPE),
        jax.ShapeDtypeStruct((bsz, N_KV_HEADS, n_t, V_ROWS, tm), MXU_DTYPE),
        jax.ShapeDtypeStruct((bsz, seq, N_KV_HEADS * HEAD_DIM), f32),
        jax.ShapeDtypeStruct((bsz, seq, N_KV_HEADS * HEAD_DIM), f32),
        jax.ShapeDtypeStruct((bsz, N_IDX_HEADS, seq, IDX_DIM), MXU_DTYPE),
        jax.ShapeDtypeStruct((bsz, seq, IDX_DIM), f32),
        jax.ShapeDtypeStruct((bsz, seq, IDX_DIM), MXU_DTYPE),
        jax.ShapeDtypeStruct((bsz, N_IDX_HEADS, seq), f32),
        jax.ShapeDtypeStruct((bsz, CONV_WIDTH - 1, CONV_DIM), f32),
    )
    tok = lambda d: pl.BlockSpec((1, tm, d), lambda b, t: (b, t, 0))
    hm = lambda n, d: pl.BlockSpec((1, n, tm, d), lambda b, t: (b, 0, t, 0))
    state = pl.BlockSpec((1, CONV_WIDTH - 1, CONV_DIM), lambda b, t: (b, 0, 0))
    return pl.pallas_call(
        functools.partial(_inproj_kernel, tm=tm, n_t=n_t, last_row=n_valid - 1 - (n_t - 1) * tm),
        out_shape=out_shape,
        grid=(bsz, n_t),
        in_specs=[
            tok(D_MODEL), state,
            pl.BlockSpec((D_MODEL, 3 * CONV_DIM), const),
            pl.BlockSpec((D_MODEL, D_QKV), const),
            pl.BlockSpec((D_MODEL, D_IDX_PAD), const),
            pl.BlockSpec((CONV_WIDTH, CONV_DIM), const),
        ],
        out_specs=(
            tok(CONV_DIM), hm(N_HEADS, N_KV_HEADS * HEAD_DIM), tok(N_KV_HEADS * HEAD_DIM),
            pl.BlockSpec((1, N_KV_HEADS, 1, V_ROWS, tm), lambda b, t: (b, 0, t, 0, 0)),
            tok(N_KV_HEADS * HEAD_DIM), tok(N_KV_HEADS * HEAD_DIM), hm(N_IDX_HEADS, IDX_DIM),
            tok(IDX_DIM), tok(IDX_DIM),
            pl.BlockSpec((1, N_IDX_HEADS, tm), lambda b, t: (b, 0, t)), state,
        ),
        scratch_shapes=[pltpu.VMEM((tm + 8, CONV_DIM), f32)],
        compiler_params=pltpu.CompilerParams(
            dimension_semantics=("arbitrary", "arbitrary"), vmem_limit_bytes=VMEM_LIMIT),
        name="inproj_conv",
    )(x, conv_prev, wa, wb, wc, conv_w)


def _ordered_key(x):
    bits = lax.bitcast_convert_type(x, jnp.int32)
    return bits ^ (lax.shift_right_arithmetic(bits, 31) & jnp.int32(0x7FFFFFFF))


def _attn_kernel(qi_ref, q_ref, wt_ref, ki_ref, kb_ref, vt_ref, nb_ref, out_ref,
                 sc_ref, hi_ref, lo_ref, mma_ref, mmb_ref, lhsa_ref, lhsb_ref, rhs_ref, rhst_ref, qit_ref, m_ref, acc_ref, *,
                 pos0, n_keys, k_top):
    tq, ck = LANES, KEY_CHUNK
    p0 = pos0 + pl.program_id(1) * tq
    n_blk = p0 // LANES + 1
    n_chunk = (n_blk * LANES + ck - 1) // ck
    q_pos = p0 + lax.broadcasted_iota(jnp.int32, (1, tq), 1)
    limit = jnp.minimum((q_pos // CHUNK + 1) * CHUNK, n_keys)
    row = lax.broadcasted_iota(jnp.int32, (ck, tq), 0)

    wt = wt_ref[0]
    qit_ref[...] = qi_ref[0].reshape(N_IDX_HEADS * tq, IDX_DIM).T

    last = n_chunk - 1
    n_pair = n_chunk // 2
    has_tail = n_chunk % 2 == 1

    def store_heads(buf, res):
        for h in range(N_HEADS):
            buf[h] = res[:, h * tq:(h + 1) * tq]

    def idx_matmul(c, buf):
        s0 = pl.multiple_of(c * ck, ck)
        store_heads(buf, _dot(ki_ref[0, pl.ds(s0, ck), :], qit_ref[...]))

    def idx_scores(c, buf):
        s0 = pl.multiple_of(c * ck, ck)
        acc = jnp.zeros((ck, tq), jnp.float32)
        for h in range(N_IDX_HEADS):
            acc = acc + jnp.maximum(buf[h], 0.0) * wt[h:h + 1, :]
        score = jnp.where(s0 + row < limit, acc, SENTINEL)
        sc_ref[pl.ds(s0, ck), :] = score
        key = _ordered_key(score)
        hi_ref[pl.ds(s0, ck), :] = lax.shift_right_arithmetic(key, 16).astype(jnp.int16)
        lo_ref[pl.ds(s0, ck), :] = ((key & jnp.int32(0xFFFF)) - 32768).astype(jnp.int16)

    idx_matmul(0, mma_ref)

    def score_pair(p, carry):
        c0 = 2 * p
        idx_matmul(c0 + 1, mmb_ref)
        idx_scores(c0, mma_ref)
        idx_matmul(jnp.minimum(c0 + 2, last), mma_ref)
        idx_scores(c0 + 1, mmb_ref)
        return carry

    lax.fori_loop(0, n_pair, score_pair, 0)

    @pl.when(has_tail)
    def _():
        idx_scores(last, mma_ref)

    def count_ge(ref, cand):
        thr = jnp.broadcast_to(cand, (16, tq)).astype(jnp.int16)

        def body(c, cnts):
            chunk = ref[pl.ds(pl.multiple_of(c * ck, ck), ck), :]
            cnts = list(cnts)
            for j in range(ck // 16):
                hit = jnp.where(chunk[j * 16:(j + 1) * 16] >= thr, jnp.int16(1), jnp.int16(0))
                cnts[j % N_COUNTERS] = cnts[j % N_COUNTERS] + hit
            return tuple(cnts)
        zero = jnp.zeros((16, tq), jnp.int16)
        cnts = lax.fori_loop(0, n_chunk, body, (zero,) * N_COUNTERS)
        cnt = sum(c.astype(jnp.int32) for c in cnts)
        return jnp.sum(cnt, axis=0, keepdims=True)

    def bisect(ref, cnt_all):
        def step(it, carry):
            off, cnt_at = carry
            cand = off | lax.shift_left(jnp.int32(1), 15 - it)
            cnt = count_ge(ref, cand - 32768)
            ok = cnt >= k_top
            return jnp.where(ok, cand, off), jnp.where(ok, cnt, cnt_at)
        off, cnt_at = lax.fori_loop(0, 16, step, (jnp.zeros((1, tq), jnp.int32), cnt_all))
        return off - 32768, cnt_at

    cnt_all = jnp.full((1, tq), 1, jnp.int32) * (n_chunk * ck)
    t_hi, cnt_hi = bisect(hi_ref, cnt_all)

    def pin_chunk(c, carry):
        sl = pl.ds(pl.multiple_of(c * ck, ck), ck)
        hi = hi_ref[sl, :]
        lo_ref[sl, :] = jnp.where(hi > t_hi[0:1].astype(jnp.int16), jnp.int16(32767),
                                  jnp.where(hi < t_hi[0:1].astype(jnp.int16), jnp.int16(-32768), lo_ref[sl, :]))
        return carry

    lax.fori_loop(0, n_chunk, pin_chunk, 0)
    t_lo, cnt_ge = bisect(lo_ref, cnt_hi)
    thr_key = lax.shift_left(t_hi, 16) | (t_lo + 32768)
    thr = lax.bitcast_convert_type(
        thr_key ^ (lax.shift_right_arithmetic(thr_key, 31) & jnp.int32(0x7FFFFFFF)), jnp.float32)
    sent_next = jnp.nextafter(jnp.float32(SENTINEL), jnp.float32(0.0))
    has_k = thr > SENTINEL
    thr = jnp.maximum(thr, sent_next)

    excess = jnp.where(has_k & (cnt_ge > k_top), 1, 0)

    @pl.when(jnp.sum(excess) > 0)
    def _():
        def count_gt(c, cnt):
            s0 = pl.multiple_of(c * ck, ck)
            return cnt + jnp.sum(jnp.where(sc_ref[pl.ds(s0, ck), :] > thr, 1, 0), axis=0, keepdims=True)
        need = k_top - lax.fori_loop(0, n_chunk, count_gt, jnp.zeros((1, tq), jnp.int32))

        def ties_before(bound):
            def body(c, cnt):
                s0 = pl.multiple_of(c * ck, ck)
                tie = (sc_ref[pl.ds(s0, ck), :] == thr) & (s0 + row < bound)
                return cnt + jnp.sum(jnp.where(tie, 1, 0), axis=0, keepdims=True)
            return lax.fori_loop(0, n_chunk, body, jnp.zeros((1, tq), jnp.int32))

        n_bits = max(1, (n_keys + ck).bit_length())

        def cut_step(it, cut):
            cand = cut | lax.shift_left(jnp.int32(1), n_bits - 1 - it)
            return jnp.where(ties_before(cand) <= need, cand, cut)

        cut = lax.fori_loop(0, n_bits, cut_step, jnp.zeros((1, tq), jnp.int32))

        def drop(c, carry):
            s0 = pl.multiple_of(c * ck, ck)
            sc = sc_ref[pl.ds(s0, ck), :]
            sc_ref[pl.ds(s0, ck), :] = jnp.where((sc == thr) & (s0 + row >= cut), SENTINEL, sc)
            return carry
        lax.fori_loop(0, n_chunk, drop, 0)

    m_ref[...] = jnp.full(m_ref.shape, M_INIT, jnp.float32)
    acc_ref[...] = jnp.zeros(acc_ref.shape, jnp.float32)

    eye = (lax.broadcasted_iota(jnp.int32, (tq, tq), 0) == lax.broadcasted_iota(jnp.int32, (tq, tq), 1))
    for h in range(N_HEADS):
        rhs_ref[h * tq:(h + 1) * tq, 0:LANES] = eye.astype(MXU_DTYPE)
        rhs_ref[h * tq:(h + 1) * tq, LANES:2 * LANES] = q_ref[0, h]
    rhst_ref[...] = rhs_ref[...].T

    def qk_matmul(c, live, lhs, buf):
        s0 = pl.multiple_of(c * ck, ck)
        sel = (sc_ref[pl.ds(s0, ck), :] >= thr) & live
        lhs[:, 0:LANES] = jnp.where(sel, 0.0, NEG).astype(MXU_DTYPE)
        lhs[:, LANES:2 * LANES] = kb_ref[0, pl.ds(s0, ck), :]
        store_heads(buf, _dot(lhs[...], rhst_ref[...]))

    def logits(buf, c, h, near):
        blk = buf[h]
        if near:
            bias = [nb_ref[jnp.clip(c * (ck // LANES) + j - (n_blk - 3), 0, 2), h] for j in range(ck // LANES)]
            blk = blk + jnp.concatenate(bias, axis=0)
        return blk

    def attend_chunk(c, c_next, buf, buf_next, lhs_next, near):
        if c_next is not None:
            qk_matmul(c_next, True, lhs_next, buf_next)
        for g in range(N_KV_HEADS):
            m_old = m_ref[g]
            m_new = jnp.concatenate(
                [jnp.maximum(m_old[:, r * tq:(r + 1) * tq],
                             jnp.max(logits(buf, c, g * GROUP + r, near), axis=0, keepdims=True))
                 for r in range(GROUP)], axis=1)
            scale = jnp.exp2(m_old - m_new)
            m_ref[g] = m_new
            probs = jnp.concatenate(
                [jnp.exp2(logits(buf, c, g * GROUP + r, near) - m_new[:, r * tq:(r + 1) * tq]).astype(MXU_DTYPE)
                 for r in range(GROUP)], axis=1)
            pv = _dot(vt_ref[0, g, c], probs)
            acc_ref[g] = scale * acc_ref[g] + pv

    qk_matmul(0, True, lhsa_ref, mma_ref)

    def attend_pair(p, near):
        c0 = 2 * p
        attend_chunk(c0, c0 + 1, mma_ref, mmb_ref, lhsb_ref, near)
        attend_chunk(c0 + 1, jnp.minimum(c0 + 2, last), mmb_ref, mma_ref, lhsa_ref, near)

    n_far = jnp.maximum(n_pair - 1, 0)

    def far_body(p, carry):
        attend_pair(p, near=False)
        return carry

    def near_body(p, carry):
        attend_pair(p, near=True)
        return carry

    lax.fori_loop(0, n_far, far_body, 0)
    lax.fori_loop(n_far, n_pair, near_body, 0)

    @pl.when(has_tail)
    def _():
        attend_chunk(last, None, mma_ref, None, None, near=True)

    for g in range(N_KV_HEADS):
        acc = acc_ref[g]
        o = acc[0:HEAD_DIM] / acc[HEAD_DIM:HEAD_DIM + 1]
        for r in range(GROUP):
            h = g * GROUP + r
            out_ref[0, :, h * HEAD_DIM:(h + 1) * HEAD_DIM] = o[:, r * tq:(r + 1) * tq].T.astype(out_ref.dtype)


def _attention(qi, q, wt, ki, kb, vt, nb, *, pos0, n_keys, k_top):
    bsz, _, seq, _ = q.shape
    tq, ck = LANES, KEY_CHUNK
    l_pad = ki.shape[1]
    n_q = seq // tq
    assert pos0 % LANES == 0 and seq % tq == 0 and l_pad % ck == 0
    assert pos0 + seq <= l_pad and n_keys <= l_pad and k_top <= ck
    qspec = lambda n, d: pl.BlockSpec((1, n, tq, d), lambda b, i: (b, 0, i, 0))
    return pl.pallas_call(
        functools.partial(_attn_kernel, pos0=pos0, n_keys=n_keys, k_top=k_top),
        out_shape=jax.ShapeDtypeStruct((bsz, seq, N_HEADS * HEAD_DIM), MXU_DTYPE),
        grid=(bsz, n_q),
        in_specs=[
            qspec(N_IDX_HEADS, IDX_DIM), qspec(N_HEADS, N_KV_HEADS * HEAD_DIM),
            pl.BlockSpec((1, N_IDX_HEADS, tq), lambda b, i: (b, 0, i)),
            pl.BlockSpec((1, l_pad, IDX_DIM), lambda b, i: (b, 0, 0)),
            pl.BlockSpec((1, l_pad, N_KV_HEADS * HEAD_DIM), lambda b, i: (b, 0, 0)),
            pl.BlockSpec((1, N_KV_HEADS, l_pad // ck, V_ROWS, ck), lambda b, i: (b, 0, 0, 0, 0)),
            pl.BlockSpec((3, N_HEADS, LANES, LANES), lambda b, i: (0, 0, 0, 0)),
        ],
        out_specs=pl.BlockSpec((1, tq, N_HEADS * HEAD_DIM), lambda b, i: (b, i, 0)),
        scratch_shapes=[
            pltpu.VMEM((l_pad, tq), jnp.float32),
            pltpu.VMEM((l_pad, tq), jnp.int16),
            pltpu.VMEM((l_pad, tq), jnp.int16),
            pltpu.VMEM((N_HEADS, ck, tq), jnp.float32),
            pltpu.VMEM((N_HEADS, ck, tq), jnp.float32),
            pltpu.VMEM((ck, 2 * LANES), MXU_DTYPE),
            pltpu.VMEM((ck, 2 * LANES), MXU_DTYPE),
            pltpu.VMEM((N_HEADS * tq, 2 * LANES), MXU_DTYPE),
            pltpu.VMEM((2 * LANES, N_HEADS * tq), MXU_DTYPE),
            pltpu.VMEM((IDX_DIM, N_IDX_HEADS * tq), MXU_DTYPE),
            pltpu.VMEM((N_KV_HEADS, 1, GROUP * tq), jnp.float32),
            pltpu.VMEM((N_KV_HEADS, V_ROWS, GROUP * tq), jnp.float32),
        ],
        compiler_params=pltpu.CompilerParams(
            dimension_semantics=("arbitrary", "arbitrary"), vmem_limit_bytes=VMEM_LIMIT),
        name="dsa_attention",
    )(qi, q, wt, ki, kb, vt, nb)


def _layernorm(x, g, b):
    mu = jnp.mean(x, axis=-1, keepdims=True)
    xc = x - mu
    var = jnp.mean(xc * xc, axis=-1, keepdims=True)
    return xc * lax.rsqrt(var + LN_EPS) * g + b


def _mlp_kernel(x_ref, co_ref, at_ref, woa_ref, wob_ref, g1_ref, b1_ref, w1_ref, w2_ref, g2_ref, b2_ref,
                out_ref, *, ff_chunk):
    mix = _dot(co_ref[...], woa_ref[...]) + _dot(at_ref[...], wob_ref[...])
    x1 = _layernorm(ALPHA * x_ref[...] + mix, g1_ref[...], b1_ref[...])
    x1b = x1.astype(MXU_DTYPE)
    ff = jnp.zeros(x1.shape, jnp.float32)
    for c in range(D_FF // ff_chunk):
        hid = jnp.maximum(_dot(x1b, w1_ref[:, c * ff_chunk:(c + 1) * ff_chunk]), 0.0)
        ff = ff + _dot((hid * hid).astype(MXU_DTYPE), w2_ref[c * ff_chunk:(c + 1) * ff_chunk, :])
    out_ref[...] = _layernorm(ALPHA * x1 + ff, g2_ref[...], b2_ref[...])


def _mlp(x, conv_out, attn, woa, wob, g1, b1, w1, w2, g2, b2, *, tm, ff_chunk=1024):
    n_tok = x.shape[0]
    tok = lambda d: pl.BlockSpec((tm, d), lambda i: (i, 0))
    const = lambda shape: pl.BlockSpec(shape, lambda i: (0, 0), pipeline_mode=pl.Buffered(1))
    return pl.pallas_call(
        functools.partial(_mlp_kernel, ff_chunk=ff_chunk),
        out_shape=jax.ShapeDtypeStruct((n_tok, D_MODEL), jnp.float32),
        grid=(n_tok // tm,),
        in_specs=[
            tok(D_MODEL), tok(CONV_DIM), tok(N_HEADS * HEAD_DIM),
            const((CONV_DIM, D_MODEL)), const((N_HEADS * HEAD_DIM, D_MODEL)),
            const((1, D_MODEL)), const((1, D_MODEL)),
            const((D_MODEL, D_FF)), const((D_FF, D_MODEL)),
            const((1, D_MODEL)), const((1, D_MODEL)),
        ],
        out_specs=tok(D_MODEL),
        compiler_params=pltpu.CompilerParams(
            dimension_semantics=("arbitrary",), vmem_limit_bytes=VMEM_LIMIT),
        name="oproj_mlp",
    )(x, conv_out, attn, woa, wob, g1, b1, w1, w2, g2, b2)


def _tiles(seq):
    seq_pad = -(-seq // LANES) * LANES
    tm_in = KEY_CHUNK if seq_pad % KEY_CHUNK == 0 else LANES
    return seq_pad, tm_in


def _mlp_tile(n_tok):
    return 512 if n_tok % 512 == 0 else n_tok


def _layer(x, conv_prev, past, pos0, lw, nb):
    bsz, seq, _ = x.shape
    wa, wb, wc, conv_w, woa, wob, g1, b1, w1, w2, g2, b2 = lw
    seq_pad, tm_in = _tiles(seq)
    xp = x if seq_pad == seq else jnp.pad(x, ((0, 0), (0, seq_pad - seq), (0, 0)))
    (conv_out, q, kb, vt, k, v, qi, ki, kib, wt, new_conv) = _inproj(
        xp, conv_prev, wa, wb, wc, conv_w, tm=tm_in, n_valid=seq)
    k, v, ki = k[:, :seq], v[:, :seq], ki[:, :seq]
    ck = KEY_CHUNK
    if past is None:
        n_keys = seq
        assert seq_pad == seq and tm_in == ck
    else:
        pk, pv, pki = past
        n_keys = pk.shape[1] + seq
        l_pad = -(-(pos0 + seq_pad) // ck) * ck
        pad = l_pad - n_keys
        kb = jnp.pad(jnp.concatenate([pk.reshape(bsz, -1, N_KV_HEADS * HEAD_DIM), k], axis=1),
                     ((0, 0), (0, pad), (0, 0))).astype(MXU_DTYPE)
        v_all = jnp.pad(jnp.concatenate([pv.reshape(bsz, -1, N_KV_HEADS * HEAD_DIM), v], axis=1),
                        ((0, 0), (0, pad), (0, 0))).astype(MXU_DTYPE)
        kib = jnp.pad(jnp.concatenate([pki, ki], axis=1), ((0, 0), (0, pad), (0, 0))).astype(MXU_DTYPE)
        v_t =jnp.transpose(v_all.reshape(bsz, l_pad // ck, ck, N_KV_HEADS, HEAD_DIM), (0, 3, 1, 4, 2))
        ones = jnp.zeros((bsz, N_KV_HEADS, l_pad // ck, V_ROWS - HEAD_DIM, ck), MXU_DTYPE).at[:, :, :, 0].set(1)
        vt = jnp.concatenate([v_t, ones], axis=3)
    k_top = min(TOPK_MAX, n_keys // 4)
    attn = _attention(qi, q, wt, kib, kb, vt, nb, pos0=pos0, n_keys=n_keys, k_top=k_top)
    n_tok = bsz * seq
    y = _mlp(x.reshape(n_tok, D_MODEL), conv_out[:, :seq].reshape(n_tok, CONV_DIM),
             attn[:, :seq].reshape(n_tok, N_HEADS * HEAD_DIM), woa, wob, g1, b1, w1, w2, g2, b2,
             tm=_mlp_tile(n_tok))
    return (y.reshape(bsz, seq, D_MODEL),
            k.reshape(bsz, seq, N_KV_HEADS, HEAD_DIM), v.reshape(bsz, seq, N_KV_HEADS, HEAD_DIM), ki, new_conv)


def _layer_weights(l, w_in, conv_w, w_o, ln1_g, ln1_b, w_ff1, w_ff2, ln2_g, ln2_b):
    wi = w_in[l].astype(MXU_DTYPE)
    wa = wi[:, :3 * CONV_DIM]
    wb = wi[:, 3 * CONV_DIM:3 * CONV_DIM + D_QKV]
    wc = jnp.pad(wi[:, 3 * CONV_DIM + D_QKV:], ((0, 0), (0, D_IDX_PAD - D_IDX)))
    wo = w_o[l].astype(MXU_DTYPE)
    row = lambda a: a[l].reshape(1, D_MODEL)
    return (wa, wb, wc, conv_w[l], wo[:CONV_DIM], wo[CONV_DIM:], row(ln1_g), row(ln1_b),
            w_ff1[l].astype(MXU_DTYPE), w_ff2[l].astype(MXU_DTYPE), row(ln2_g), row(ln2_b))


def kernel(x_prompt, x_sample, cache_k, cache_v, cache_kidx, state_conv, w_in, conv_w, w_o, ln1_g, ln1_b,
           w_ff1, w_ff2, ln2_g, ln2_b, rel_bias):
    bp = x_prompt.shape[0]
    past_len = cache_k.shape[2]
    nb = _bias_tiles(rel_bias)
    hp, hs = x_prompt, x_sample
    outs = [[] for _ in range(8)]
    for l in range(DEPTH):
        lw = _layer_weights(l, w_in, conv_w, w_o, ln1_g, ln1_b, w_ff1, w_ff2, ln2_g, ln2_b)
        zero_conv = jnp.zeros((bp, CONV_WIDTH - 1, CONV_DIM), jnp.float32)
        hp, k_n, v_n, ki_n, c_n = _layer(hp, zero_conv, None, 0, lw, nb)
        for o, a in zip(outs[:4], (k_n, v_n, ki_n, c_n)):
            o.append(a)
        hs, k_n, v_n, ki_n, c_n = _layer(hs, state_conv[l], (cache_k[l], cache_v[l], cache_kidx[l]),
                                         past_len, lw, nb)
        for o, a in zip(outs[4:], (k_n, v_n, ki_n, c_n)):
            o.append(a)
    return (hp, hs) + tuple(jnp.stack(o) for o in outs)
```
